```python
import jax, jax.numpy as jnp
from jax import lax
import numpy as np

D_MODEL = 4096
BATCH = 4
SEQ = 4096
DEPTH = 4

HEAD_DIM = 128
N_FOX_HEADS = D_MODEL // (2 * HEAD_DIM)
N_DIL_HEADS = D_MODEL // (2 * HEAD_DIM)
FOX_WIDTH = N_FOX_HEADS * HEAD_DIM
DIL_WIDTH = N_DIL_HEADS * HEAD_DIM
MIX_WIDTH = FOX_WIDTH + DIL_WIDTH
IN_COLS = 3 * FOX_WIDTH + N_FOX_HEADS + 3 * DIL_WIDTH
SPLIT_POINTS = (
    FOX_WIDTH,
    2 * FOX_WIDTH,
    3 * FOX_WIDTH,
    3 * FOX_WIDTH + N_FOX_HEADS,
    3 * FOX_WIDTH + N_FOX_HEADS + DIL_WIDTH,
    3 * FOX_WIDTH + N_FOX_HEADS + 2 * DIL_WIDTH,
)
DILATION_CONFIGS = ((128, 1), (512, 4), (2048, 16))
Q_BLOCK = 128
ROPE_THETA = 500000.0
ROPE_DIM = HEAD_DIM // 4
N_MEM = 256
N_XHEADS = 4
XHEAD_DIM = 128
X_WIDTH = N_XHEADS * XHEAD_DIM
D_FF = 4 * D_MODEL
DN_ALPHA = (2 * DEPTH) ** 0.25
DN_BETA = (8 * DEPTH) ** -0.25
LN_EPS = 1e-5
NEG_INF = -1e30

kernel_name = "hybrid_fox_dilated_deepnorm_trunk"

F32 = jnp.float32


def layer_norm(x, g, b):
    xf = x.astype(F32)
    mu = jnp.mean(xf, axis=-1, keepdims=True)
    var = jnp.mean(jnp.square(xf - mu), axis=-1, keepdims=True)
    y = (xf - mu) * lax.rsqrt(var + LN_EPS) * g.astype(F32) + b.astype(F32)
    return y.astype(x.dtype)


def head_rms_norm(o, g):
    nh, hd = o.shape[-2], o.shape[-1]
    of = o.astype(F32)
    y = of * lax.rsqrt(jnp.mean(jnp.square(of), axis=-1, keepdims=True) + LN_EPS)
    return (y * g.astype(F32).reshape(nh, hd)).astype(o.dtype)


def rope(t, positions):
    inv_freq = ROPE_THETA ** (-jnp.arange(0, ROPE_DIM, 2, dtype=F32) / ROPE_DIM)
    ang = positions.astype(F32)[..., None] * inv_freq
    cos = jnp.cos(ang)[:, :, None, :]
    sin = jnp.sin(ang)[:, :, None, :]
    half = ROPE_DIM // 2
    tr = t[..., :ROPE_DIM].astype(F32)
    t1, t2 = tr[..., :half], tr[..., half:]
    rot = jnp.concatenate([t1 * cos - t2 * sin, t2 * cos + t1 * sin], axis=-1).astype(t.dtype)
    return jnp.concatenate([rot, t[..., ROPE_DIM:]], axis=-1)


def forgetting_attention(q, k, v, log_f):
    bsz, seq, nh, hd = q.shape
    nb = seq // Q_BLOCK
    scale = hd ** -0.5
    c = jnp.cumsum(log_f, axis=1)
    c_k = jnp.transpose(c, (0, 2, 1))[:, :, None, :]
    q_blocks = jnp.moveaxis(q.reshape(bsz, nb, Q_BLOCK, nh, hd), 1, 0)
    c_blocks = jnp.moveaxis(c.reshape(bsz, nb, Q_BLOCK, nh), 1, 0)
    key_pos = jnp.arange(seq)

    def block(args):
        q_n, c_n, n = args
        s = jnp.einsum('bqhd,bkhd->bhqk', q_n, k).astype(F32) * scale
        s = s + jnp.transpose(c_n, (0, 2, 1))[..., None] - c_k
        q_pos = n * Q_BLOCK + jnp.arange(Q_BLOCK)
        mask = key_pos[None, :] <= q_pos[:, None]
        s = jnp.where(mask, s, NEG_INF)
        p = jax.nn.softmax(s, axis=-1)
        return jnp.einsum('bhqk,bkhd->bqhd', p.astype(v.dtype), v)

    out = lax.map(block, (q_blocks, c_blocks, jnp.arange(nb)))
    return jnp.moveaxis(out, 0, 1).reshape(bsz, seq, nh, hd)


def dilated_branch(q, k, v, window, dilation):
    bsz, seq, nh, hd = q.shape
    n_back = window // dilation
    sub_len = seq // dilation
    nb = -(-sub_len // Q_BLOCK)
    padded = nb * Q_BLOCK
    n_sub = bsz * dilation
    scale = hd ** -0.5

    def to_sub(t):
        t = t.reshape(bsz, sub_len, dilation, nh, hd).transpose(0, 2, 1, 3, 4)
        return t.reshape(n_sub, sub_len, nh, hd)

    def band(t):
        tp = jnp.pad(t, ((0, 0), (Q_BLOCK, padded - sub_len), (0, 0), (0, 0)))
        tp = tp.reshape(n_sub, nb + 1, Q_BLOCK, nh, hd)
        return jnp.concatenate([tp[:, :-1], tp[:, 1:]], axis=2)

    qs = jnp.pad(to_sub(q), ((0, 0), (0, padded - sub_len), (0, 0), (0, 0)))
    qb = qs.reshape(n_sub, nb, Q_BLOCK, nh, hd)
    kb = band(to_sub(k))
    vb = band(to_sub(v))

    s = jnp.einsum('nbqhd,nbkhd->nbhqk', qb, kb).astype(F32) * scale
    qi = jnp.arange(Q_BLOCK)[:, None]
    ki = jnp.arange(2 * Q_BLOCK)[None, :]
    dist = Q_BLOCK + qi - ki
    blk = jnp.arange(nb)[:, None, None]
    valid = (dist >= 0) & (dist <= n_back) & (blk * Q_BLOCK + ki - Q_BLOCK >= 0)
    s = jnp.where(valid[None, :, None], s, NEG_INF)
    lse = jax.nn.logsumexp(s, axis=-1)
    p = jnp.exp(s - lse[..., None])
    o = jnp.einsum('nbhqk,nbkhd->nbqhd', p.astype(v.dtype), vb).astype(F32)

    o = o.reshape(n_sub, padded, nh, hd)[:, :sub_len]
    o = o.reshape(bsz, dilation, sub_len, nh, hd).transpose(0, 2, 1, 3, 4).reshape(bsz, seq, nh, hd)
    lse = jnp.transpose(lse, (0, 1, 3, 2)).reshape(n_sub, padded, nh)[:, :sub_len]
    lse = lse.reshape(bsz, dilation, sub_len, nh).transpose(0, 2, 1, 3).reshape(bsz, seq, nh)
    return o, lse


def dilated_attention(q, k, v):
    outs, lses = [], []
    for window, dilation in DILATION_CONFIGS:
        o, lse = dilated_branch(q, k, v, window, dilation)
        outs.append(o)
        lses.append(lse)
    w = jax.nn.softmax(jnp.stack(lses, axis=0), axis=0)
    y = jnp.einsum('cbsh,cbshd->bshd', w, jnp.stack(outs, axis=0))
    return y.astype(q.dtype)


def hybrid_mixer(h, positions, w_in, b_forget, g_fox, g_dil, w_out):
    bsz, seq, _ = h.shape
    proj = h @ w_in
    q_f, k_f, v_f, f_logit, q_d, k_d, v_d = jnp.split(proj, SPLIT_POINTS, axis=-1)

    def heads(t):
        return t.reshape(bsz, seq, -1, HEAD_DIM)

    log_f = jax.nn.log_sigmoid(f_logit.astype(F32) + b_forget.astype(F32))
    o_fox = forgetting_attention(heads(q_f), heads(k_f), heads(v_f), log_f)

    o_dil = dilated_attention(rope(heads(q_d), positions), rope(heads(k_d), positions), heads(v_d))

    y = jnp.concatenate([
        head_rms_norm(o_fox, g_fox).reshape(bsz, seq, FOX_WIDTH),
        head_rms_norm(o_dil, g_dil).reshape(bsz, seq, DIL_WIDTH),
    ], axis=-1)
    return y @ w_out


def memory_cross_attention(h, mem, w_xq, w_xk, w_xv, w_xo):
    bsz, seq, _ = h.shape
    n_mem = mem.shape[1]
    q = (h @ w_xq).reshape(bsz, seq, N_XHEADS, XHEAD_DIM)
    k = (mem @ w_xk).reshape(bsz, n_mem, N_XHEADS, XHEAD_DIM)
    v = (mem @ w_xv).reshape(bsz, n_mem, N_XHEADS, XHEAD_DIM)
    s = jnp.einsum('bqhd,bkhd->bhqk', q, k).astype(F32) * (XHEAD_DIM ** -0.5)
    p = jax.nn.softmax(s, axis=-1)
    o = jnp.einsum('bhqk,bkhd->bqhd', p.astype(v.dtype), v).reshape(bsz, seq, X_WIDTH)
    return o @ w_xo


def squared_relu_mlp(h, w_up, w_down):
    return jnp.square(jax.nn.relu(h @ w_up)) @ w_down


def setup_inputs(seed: int = 0) -> dict:
    key = jax.random.key(seed)
    ks = jax.random.split(key, 24)
    nrm = lambda k, shape, scale: jax.random.normal(k, shape, F32) * scale
    x = nrm(ks[0], (BATCH, SEQ, D_MODEL), 1.0)
    mem = nrm(ks[1], (BATCH, N_MEM, D_MODEL), 1.0)
    offsets = jax.random.randint(ks[2], (BATCH, 1), 0, 4096, dtype=jnp.int32)
    positions = (offsets + jnp.arange(SEQ, dtype=jnp.int32)[None, :]).astype(jnp.int32)

    w_in = nrm(ks[3], (DEPTH, D_MODEL, IN_COLS), D_MODEL ** -0.5)
    b_forget = (jnp.linspace(1.0, 4.0, N_FOX_HEADS, dtype=F32)[None, :]
                + nrm(ks[4], (DEPTH, N_FOX_HEADS), 0.1))
    g_fox = 1.0 + nrm(ks[5], (DEPTH, FOX_WIDTH), 0.02)
    g_dil = 1.0 + nrm(ks[6], (DEPTH, DIL_WIDTH), 0.02)
    w_out = nrm(ks[7], (DEPTH, MIX_WIDTH, D_MODEL), DN_BETA * MIX_WIDTH ** -0.5)
    ln1_g = 1.0 + nrm(ks[8], (DEPTH, D_MODEL), 0.02)
    ln1_b = nrm(ks[9], (DEPTH, D_MODEL), 0.02)

    w_xq = nrm(ks[10], (DEPTH, D_MODEL, X_WIDTH), D_MODEL ** -0.5)
    w_xk = nrm(ks[11], (DEPTH, D_MODEL, X_WIDTH), D_MODEL ** -0.5)
    w_xv = nrm(ks[12], (DEPTH, D_MODEL, X_WIDTH), D_MODEL ** -0.5)
    w_xo = nrm(ks[13], (DEPTH, X_WIDTH, D_MODEL), DN_BETA * X_WIDTH ** -0.5)
    ln2_g = 1.0 + nrm(ks[14], (DEPTH, D_MODEL), 0.02)
    ln2_b = nrm(ks[15], (DEPTH, D_MODEL), 0.02)

    w_up = nrm(ks[16], (DEPTH, D_MODEL, D_FF), D_MODEL ** -0.5)
    w_down = nrm(ks[17], (DEPTH, D_FF, D_MODEL), DN_BETA * D_FF ** -0.5)
    ln3_g = 1.0 + nrm(ks[18], (DEPTH, D_MODEL), 0.02)
    ln3_b = nrm(ks[19], (DEPTH, D_MODEL), 0.02)
    return {
        "x": x, "mem": mem, "positions": positions,
        "w_in": w_in, "b_forget": b_forget, "g_fox": g_fox, "g_dil": g_dil, "w_out": w_out,
        "ln1_g": ln1_g, "ln1_b": ln1_b,
        "w_xq": w_xq, "w_xk": w_xk, "w_xv": w_xv, "w_xo": w_xo,
        "ln2_g": ln2_g, "ln2_b": ln2_b,
        "w_up": w_up, "w_down": w_down, "ln3_g": ln3_g, "ln3_b": ln3_b,
    }


def reference(x, mem, positions, w_in, b_forget, g_fox, g_dil, w_out, ln1_g, ln1_b,
              w_xq, w_xk, w_xv, w_xo, ln2_g, ln2_b, w_up, w_down, ln3_g, ln3_b):
    for l in range(DEPTH):
        x = layer_norm(DN_ALPHA * x + hybrid_mixer(x, positions, w_in[l], b_forget[l],
                                                   g_fox[l], g_dil[l], w_out[l]),
                       ln1_g[l], ln1_b[l])
        x = layer_norm(DN_ALPHA * x + memory_cross_attention(x, mem, w_xq[l], w_xk[l],
                                                             w_xv[l], w_xo[l]),
                       ln2_g[l], ln2_b[l])
        x = layer_norm(DN_ALPHA * x + squared_relu_mlp(x, w_up[l], w_down[l]),
                       ln3_g[l], ln3_b[l])
    return x
```

```python
import functools
import math

import jax
import jax.numpy as jnp
import numpy as np
from jax import lax
from jax.experimental import pallas as pl
from jax.experimental.pallas import tpu as pltpu

F32 = jnp.float32
BF16 = jnp.bfloat16

HEAD_DIM = 128
ROPE_DIM = HEAD_DIM // 4
ROPE_HALF = ROPE_DIM // 2
ROPE_THETA = 500000.0
DILATION_CONFIGS = ((128, 1), (512, 4), (2048, 16))
MAX_WINDOW = max(w for w, _ in DILATION_CONFIGS)
N_XHEADS = 4
LN_EPS = 1e-5
NEG_INF = -1e30
ATTN_SCALE = HEAD_DIM ** -0.5

V7X_VMEM_BYTES = 64 * 2**20
LANES = 128


def _nbytes(shape, dtype):
    return math.prod(shape) * jnp.dtype(dtype).itemsize


def _params(semantics, blocks, temps=0):
    need = 2 * sum(_nbytes(s, d) for s, d in blocks) + temps
    limit = min(need + need // 4 + (2 << 20), V7X_VMEM_BYTES - (4 << 20))
    return pltpu.CompilerParams(dimension_semantics=semantics, vmem_limit_bytes=int(limit))


def _tile(dim, pref):
    t = min(dim, pref)
    while dim % t:
        t //= 2
    return t


def _rope_table_kernel(pos_ref, inv_ref, cos_ref, sa_ref, sb_ref):
    ang = pos_ref[...].astype(F32) * inv_ref[...]
    lane = lax.broadcasted_iota(jnp.int32, ang.shape, 1)
    cos = jnp.cos(ang)
    sin = jnp.sin(ang)
    cos_ref[...] = cos
    sa_ref[...] = jnp.where(lane < ROPE_HALF, -sin, 0.0)
    sb_ref[...] = jnp.where((lane >= ROPE_HALF) & (lane < ROPE_DIM), sin, 0.0)


def _rope_tables(positions):
    t = positions.size
    rows = _tile(t, 2048)
    inv = ROPE_THETA ** (-jnp.arange(0, ROPE_DIM, 2, dtype=F32) / ROPE_DIM)
    inv_lane = jnp.concatenate([inv, inv, jnp.zeros((LANES - ROPE_DIM,), F32)]).reshape(1, LANES)
    out = jax.ShapeDtypeStruct((t, LANES), F32)
    blk = pl.BlockSpec((rows, LANES), lambda i: (i, 0))
    return pl.pallas_call(
        _rope_table_kernel,
        grid=(t // rows,),
        in_specs=[pl.BlockSpec((rows, 1), lambda i: (i, 0)),
                  pl.BlockSpec((1, LANES), lambda i: (0, 0))],
        out_specs=[blk, blk, blk],
        out_shape=[out, out, out],
        compiler_params=_params(("parallel",), [((rows, LANES), F32)] * 4, temps=8 * rows * LANES * 4),
        name="rope_tables",
    )(positions.reshape(t, 1), inv_lane)


def _inproj_kernel(x_ref, w_ref, cos_ref, sa_ref, sb_ref, o_ref, *, tiles_per_seg):
    seg = pl.program_id(1) // tiles_per_seg
    acc = jnp.dot(x_ref[...], w_ref[...], preferred_element_type=F32)
    bn = acc.shape[1]

    @pl.when(seg == 0)
    def _():
        o_ref[...] = (acc * ATTN_SCALE).astype(o_ref.dtype)

    @pl.when((seg == 1) | (seg == 2) | (seg == 5))
    def _():
        o_ref[...] = acc.astype(o_ref.dtype)

    @pl.when((seg == 3) | (seg == 4))
    def _():
        mult = jnp.where(seg == 3, ATTN_SCALE, 1.0).astype(F32)
        cos = cos_ref[...] * mult
        sa = sa_ref[...] * mult
        sb = sb_ref[...] * mult
        for c in range(bn // HEAD_DIM):
            blk = acc[:, c * HEAD_DIM:(c + 1) * HEAD_DIM]
            rot = (blk * cos
                   + pltpu.roll(blk, HEAD_DIM - ROPE_HALF, 1) * sa
                   + pltpu.roll(blk, ROPE_HALF, 1) * sb)
            o_ref[:, c * HEAD_DIM:(c + 1) * HEAD_DIM] = rot.astype(o_ref.dtype)


def _in_projection(xb, w, cos, sa, sb, seg_width):
    m, k = xb.shape
    n = w.shape[1]
    bm = _tile(m, 1024)
    bn = _tile(seg_width, 1024)
    tab = pl.BlockSpec((bm, LANES), lambda i, j: (i, 0))
    return pl.pallas_call(
        functools.partial(_inproj_kernel, tiles_per_seg=seg_width // bn),
        grid=(m // bm, n // bn),
        in_specs=[pl.BlockSpec((bm, k), lambda i, j: (i, 0)),
                  pl.BlockSpec((k, bn), lambda i, j: (0, j)),
                  tab, tab, tab],
        out_specs=pl.BlockSpec((bm, bn), lambda i, j: (i, j)),
        out_shape=jax.ShapeDtypeStruct((m, n), BF16),
        compiler_params=_params(
            ("parallel", "arbitrary"),
            [((bm, k), BF16), ((k, bn), BF16), ((bm, bn), BF16)] + [((bm, LANES), F32)] * 3,
            temps=2 * bm * bn * 4),
        name="in_projection",
    )(xb, w, cos, sa, sb)


def _forget_kernel(x_ref, w_ref, b_ref, c_ref, carry_ref):
    @pl.when(pl.program_id(1) == 0)
    def _():
        carry_ref[...] = jnp.zeros_like(carry_ref)

    z = lax.dot_general(w_ref[...], x_ref[...], (((1,), (1,)), ((), ())),
                        preferred_element_type=F32) + b_ref[...]
    log_f = jnp.minimum(z, 0.0) - jnp.log1p(jnp.exp(-jnp.abs(z)))
    ts = z.shape[1]
    row = lax.broadcasted_iota(jnp.int32, (ts, ts), 0)
    col = lax.broadcasted_iota(jnp.int32, (ts, ts), 1)
    upper = (row <= col).astype(BF16)
    hi = log_f.astype(BF16)
    r1 = log_f - hi.astype(F32)
    mid = r1.astype(BF16)
    lo = (r1 - mid.astype(F32)).astype(BF16)
    cum = (jnp.dot(hi, upper, preferred_element_type=F32)
           + jnp.dot(mid, upper, preferred_element_type=F32)
           + jnp.dot(lo, upper, preferred_element_type=F32))
    c = cum + carry_ref[:, 0:1]
    c_ref[...] = c
    carry_ref[...] = jnp.broadcast_to(c[:, ts - 1:ts], carry_ref.shape)


def _forget_cumsum(xb, w_ft, b_f, bsz, seq):
    nh, k = w_ft.shape
    ts = _tile(seq, 512)
    ns = seq // ts
    return pl.pallas_call(
        _forget_kernel,
        grid=(bsz, ns),
        in_specs=[pl.BlockSpec((ts, k), lambda b, s: (b * ns + s, 0)),
                  pl.BlockSpec((nh, k), lambda b, s: (0, 0)),
                  pl.BlockSpec((nh, 1), lambda b, s: (0, 0))],
        out_specs=pl.BlockSpec((None, nh, ts), lambda b, s: (b, 0, s)),
        out_shape=jax.ShapeDtypeStruct((bsz, nh, seq), F32),
        scratch_shapes=[pltpu.VMEM((nh, LANES), F32)],
        compiler_params=_params(
            ("parallel", "arbitrary"),
            [((ts, k), BF16), ((nh, k), BF16), ((nh, ts), F32)],
            temps=4 * ts * ts * 4),
        name="forget_cumsum",
    )(xb, w_ft, b_f.reshape(nh, 1))


def _online_softmax_step(q, k, v, bias, carry):
    m, l, acc = carry
    s = lax.dot_general(q, k, (((1,), (1,)), ((), ())), preferred_element_type=F32) + bias
    m_new = jnp.maximum(m, jnp.max(s, axis=1, keepdims=True))
    p = jnp.exp(s - m_new)
    alpha = jnp.exp(m - m_new)
    l = alpha * l + jnp.sum(p, axis=1, keepdims=True)
    acc = alpha * acc + jnp.dot(p.astype(v.dtype), v, preferred_element_type=F32)
    return m_new, l, acc


def _finish_head(carry, g_ref, o_ref):
    _, l, acc = carry
    o = acc / l
    y = o * lax.rsqrt(jnp.mean(o * o, axis=1, keepdims=True) + LN_EPS)
    o_ref[...] = (y * g_ref[...]).astype(o_ref.dtype)


def _init_carry(tq):
    return (jnp.full((tq, 1), NEG_INF, F32), jnp.zeros((tq, 1), F32), jnp.zeros((tq, HEAD_DIM), F32))


def _fox_kernel(q_ref, k_ref, v_ref, c_ref, g_ref, o_ref):
    qi = pl.program_id(2)
    q = q_ref[...]
    tq = q.shape[0]

    def tile(ki):
        start = pl.multiple_of(ki * tq, tq)
        return (k_ref[pl.ds(start, tq), :], v_ref[pl.ds(start, tq), :], -c_ref[:, pl.ds(start, tq)])

    def below_diagonal(ki, carry):
        k, v, bias = tile(ki)
        return _online_softmax_step(q, k, v, bias, carry)

    carry = lax.fori_loop(0, qi, below_diagonal, _init_carry(tq))
    k, v, bias = tile(qi)
    row = lax.broadcasted_iota(jnp.int32, (tq, tq), 0)
    col = lax.broadcasted_iota(jnp.int32, (tq, tq), 1)
    carry = _online_softmax_step(q, k, v, jnp.where(col <= row, bias, NEG_INF), carry)
    _finish_head(carry, g_ref, o_ref)


def _fox_attention(qkv, c, g, bsz, seq, nh):
    tq = _tile(seq, 512)
    nq = seq // tq
    return pl.pallas_call(
        _fox_kernel,
        grid=(bsz, nh, nq),
        in_specs=[pl.BlockSpec((tq, HEAD_DIM), lambda b, h, i: (b * nq + i, h)),
                  pl.BlockSpec((seq, HEAD_DIM), lambda b, h, i: (b, nh + h)),
                  pl.BlockSpec((seq, HEAD_DIM), lambda b, h, i: (b, 2 * nh + h)),
                  pl.BlockSpec((None, None, 1, seq), lambda b, h, i: (b, h, 0, 0)),
                  pl.BlockSpec((1, HEAD_DIM), lambda b, h, i: (0, h))],
        out_specs=pl.BlockSpec((tq, HEAD_DIM), lambda b, h, i: (b * nq + i, h)),
        out_shape=jax.ShapeDtypeStruct((bsz * seq, nh * HEAD_DIM), BF16),
        compiler_params=_params(
            ("parallel", "parallel", "arbitrary"),
            [((tq, HEAD_DIM), BF16)] * 2 + [((seq, HEAD_DIM), BF16)] * 2 + [((8, seq), F32)],
            temps=6 * tq * tq * 4),
        name="fox_attention",
    )(qkv, qkv, qkv, c.reshape(bsz, nh, 1, seq), g.reshape(1, nh * HEAD_DIM))


def _dilated_bias_tables(tq):
    n_off = MAX_WINDOW // tq + 1
    off = np.arange(n_off)[:, None, None] * tq
    dist = off + np.arange(tq)[None, :, None] - np.arange(tq)[None, None, :]
    count = np.zeros(dist.shape, np.float64)
    for window, dilation in DILATION_CONFIGS:
        count += (dist >= 0) & (dist <= window) & (dist % dilation == 0)
    return np.where(count > 0, np.log(np.maximum(count, 1.0)), NEG_INF).astype(np.float32)


def _dilated_kernel(q_ref, k_ref, v_ref, bias_ref, g_ref, o_ref):
    qi = pl.program_id(2)
    q = q_ref[...]
    tq = q.shape[0]
    n_off = bias_ref.shape[0]

    def step(off, carry):
        start = pl.multiple_of((qi - off) * tq, tq)
        return _online_softmax_step(q, k_ref[pl.ds(start, tq), :], v_ref[pl.ds(start, tq), :],
                                    bias_ref[off], carry)

    carry = lax.fori_loop(0, jnp.minimum(qi + 1, n_off), step, _init_carry(tq))
    _finish_head(carry, g_ref, o_ref)


def _dilated_attention(qkv, g, bsz, seq, nh, col0):
    tq = _tile(seq, 256)
    nq = seq // tq
    bias = jnp.asarray(_dilated_bias_tables(tq))
    n_off = bias.shape[0]
    return pl.pallas_call(
        _dilated_kernel,
        grid=(bsz, nh, nq),
        in_specs=[pl.BlockSpec((tq, HEAD_DIM), lambda b, h, i: (b * nq + i, col0 + h)),
                  pl.BlockSpec((seq, HEAD_DIM), lambda b, h, i: (b, col0 + nh + h)),
                  pl.BlockSpec((seq, HEAD_DIM), lambda b, h, i: (b, col0 + 2 * nh + h)),
                  pl.BlockSpec((n_off, tq, tq), lambda b, h, i: (0, 0, 0)),
                  pl.BlockSpec((1, HEAD_DIM), lambda b, h, i: (0, h))],
        out_specs=pl.BlockSpec((tq, HEAD_DIM), lambda b, h, i: (b * nq + i, h)),
        out_shape=jax.ShapeDtypeStruct((bsz * seq, nh * HEAD_DIM), BF16),
        compiler_params=_params(
            ("parallel", "parallel", "arbitrary"),
            [((tq, HEAD_DIM), BF16)] * 2 + [((seq, HEAD_DIM), BF16)] * 2 + [((n_off, tq, tq), F32)],
            temps=6 * tq * tq * 4),
        name="dilated_attention",
    )(qkv, qkv, qkv, bias, g.reshape(1, nh * HEAD_DIM))


def _outproj_kernel(a_ref, b_ref, wa_ref, wb_ref, o_ref):
    o_ref[...] = (jnp.dot(a_ref[...], wa_ref[...], preferred_element_type=F32)
                  + jnp.dot(b_ref[...], wb_ref[...], preferred_element_type=F32))


def _out_projection(ya, yb, w):
    m, ka = ya.shape
    kb = yb.shape[1]
    assert ka == kb and w.shape[0] == ka + kb
    n = w.shape[1]
    bm = _tile(m, 1024)
    bn = _tile(n, 1024)
    return pl.pallas_call(
        _outproj_kernel,
        grid=(m // bm, n // bn),
        in_specs=[pl.BlockSpec((bm, ka), lambda i, j: (i, 0)),
                  pl.BlockSpec((bm, kb), lambda i, j: (i, 0)),
                  pl.BlockSpec((ka, bn), lambda i, j: (0, j)),
                  pl.BlockSpec((kb, bn), lambda i, j: (1, j))],
        out_specs=pl.BlockSpec((bm, bn), lambda i, j: (i, j)),
        out_shape=jax.ShapeDtypeStruct((m, n), F32),
        compiler_params=_params(
            ("parallel", "arbitrary"),
            [((bm, ka), BF16)] * 2 + [((ka, bn), BF16)] * 2 + [((bm, bn), F32)],
            temps=bm * bn * 4),
        name="out_projection",
    )(ya, yb, w, w)


def _matmul_kernel(x_ref, w_ref, o_ref, *, activation):
    acc = jnp.dot(x_ref[...], w_ref[...], preferred_element_type=F32)
    if activation == "relu2":
        r = jnp.maximum(acc, 0.0)
        acc = r * r
    o_ref[...] = acc.astype(o_ref.dtype)


def _matmul(x, w, out_dtype, activation=None, name="matmul"):
    m, k = x.shape
    n = w.shape[1]
    bm = _tile(m, 1024)
    bn = _tile(n, 1024)
    return pl.pallas_call(
        functools.partial(_matmul_kernel, activation=activation),
        grid=(m // bm, n // bn),
        in_specs=[pl.BlockSpec((bm, k), lambda i, j: (i, 0)),
                  pl.BlockSpec((k, bn), lambda i, j: (0, j))],
        out_specs=pl.BlockSpec((bm, bn), lambda i, j: (i, j)),
        out_shape=jax.ShapeDtypeStruct((m, n), out_dtype),
        compiler_params=_params(
            ("parallel", "arbitrary"),
            [((bm, k), BF16), ((k, bn), BF16), ((bm, bn), out_dtype)],
            temps=2 * bm * bn * 4),
        name=name,
    )(x, w)


def _matmul_kacc_kernel(x_ref, w_ref, o_ref):
    @pl.when(pl.program_id(2) == 0)
    def _():
        o_ref[...] = jnp.zeros_like(o_ref)

    o_ref[...] += jnp.dot(x_ref[...], w_ref[...], preferred_element_type=F32)


def _matmul_kacc(x, w, name):
    m, k = x.shape
    n = w.shape[1]
    bm = _tile(m, 1024)
    bn = _tile(n, 1024)
    bk = _tile(k, 2048)
    return pl.pallas_call(
        _matmul_kacc_kernel,
        grid=(m // bm, n // bn, k // bk),
        in_specs=[pl.BlockSpec((bm, bk), lambda i, j, kk: (i, kk)),
                  pl.BlockSpec((bk, bn), lambda i, j, kk: (kk, j))],
        out_specs=pl.BlockSpec((bm, bn), lambda i, j, kk: (i, j)),
        out_shape=jax.ShapeDtypeStruct((m, n), F32),
        compiler_params=_params(
            ("parallel", "parallel", "arbitrary"),
            [((bm, bk), BF16), ((bk, bn), BF16), ((bm, bn), F32)],
            temps=bm * bn * 4),
        name=name,
    )(x, w)


def _layer_norm_rows(z, g, b):
    mu = jnp.mean(z, axis=1, keepdims=True)
    d = z - mu
    var = jnp.mean(d * d, axis=1, keepdims=True)
    return d * lax.rsqrt(var + LN_EPS) * g + b


def _residual_ln_kernel(x_ref, f_ref, g_ref, b_ref, o_ref, ob_ref, *, alpha):
    y = _layer_norm_rows(alpha * x_ref[...] + f_ref[...], g_ref[...], b_ref[...])
    o_ref[...] = y
    ob_ref[...] = y.astype(ob_ref.dtype)


def _residual_ln(x, f, g, b, alpha):
    m, d = x.shape
    bm = _tile(m, 256)
    row = pl.BlockSpec((bm, d), lambda i: (i, 0))
    vec = pl.BlockSpec((1, d), lambda i: (0, 0))
    return pl.pallas_call(
        functools.partial(_residual_ln_kernel, alpha=alpha),
        grid=(m // bm,),
        in_specs=[row, row, vec, vec],
        out_specs=[row, row],
        out_shape=[jax.ShapeDtypeStruct((m, d), F32), jax.ShapeDtypeStruct((m, d), BF16)],
        compiler_params=_params(("parallel",), [((bm, d), F32)] * 3 + [((bm, d), BF16)],
                                temps=3 * bm * d * 4),
        name="residual_layer_norm",
    )(x, f, g.reshape(1, d), b.reshape(1, d))


def _cross_kernel(xb_ref, x_ref, kv_ref, wq_ref, wo_ref, g_ref, b_ref, o_ref, ob_ref, *, alpha):
    q = (jnp.dot(xb_ref[...], wq_ref[...], preferred_element_type=F32) * ATTN_SCALE).astype(BF16)
    width = N_XHEADS * HEAD_DIM
    heads = []
    for h in range(N_XHEADS):
        lo, hi = h * HEAD_DIM, (h + 1) * HEAD_DIM
        s = lax.dot_general(q[:, lo:hi], kv_ref[:, lo:hi], (((1,), (1,)), ((), ())),
                            preferred_element_type=F32)
        p = jnp.exp(s - jnp.max(s, axis=1, keepdims=True))
        o = jnp.dot(p.astype(BF16), kv_ref[:, width + lo:width + hi], preferred_element_type=F32)
        heads.append((o / jnp.sum(p, axis=1, keepdims=True)).astype(BF16))
    f = jnp.dot(jnp.concatenate(heads, axis=1), wo_ref[...], preferred_element_type=F32)
    y = _layer_norm_rows(alpha * x_ref[...] + f, g_ref[...], b_ref[...])
    o_ref[...] = y
    ob_ref[...] = y.astype(ob_ref.dtype)


def _cross_attention(xb, x, kv, wq, wo, g, b, alpha, seq):
    m, d = x.shape
    n_mem = kv.shape[0] // (m // seq)
    width = N_XHEADS * HEAD_DIM
    bm = _tile(seq, 256)
    per_batch = seq // bm
    row_b = pl.BlockSpec((bm, d), lambda i: (i, 0))
    vec = pl.BlockSpec((1, d), lambda i: (0, 0))
    return pl.pallas_call(
        functools.partial(_cross_kernel, alpha=alpha),
        grid=(m // bm,),
        in_specs=[row_b, row_b,
                  pl.BlockSpec((n_mem, 2 * width), lambda i: (i // per_batch, 0)),
                  pl.BlockSpec((d, width), lambda i: (0, 0)),
                  pl.BlockSpec((width, d), lambda i: (0, 0)),
                  vec, vec],
        out_specs=[row_b, row_b],
        out_shape=[jax.ShapeDtypeStruct((m, d), F32), jax.ShapeDtypeStruct((m, d), BF16)],
        compiler_params=_params(
            ("parallel",),
            [((bm, d), BF16)] * 2 + [((bm, d), F32)] * 2
            + [((n_mem, 2 * width), BF16), ((d, width), BF16), ((width, d), BF16)],
            temps=4 * bm * d * 4),
        name="cross_attention",
    )(xb, x, kv, wq, wo, g.reshape(1, d), b.reshape(1, d))


def kernel(x, mem, positions, w_in, b_forget, g_fox, g_dil, w_out, ln1_g, ln1_b,
           w_xq, w_xk, w_xv, w_xo, ln2_g, ln2_b, w_up, w_down, ln3_g, ln3_b):
    bsz, seq, d_model = x.shape
    depth = w_in.shape[0]
    nh = d_model // (2 * HEAD_DIM)
    width = nh * HEAD_DIM
    alpha = (2 * depth) ** 0.25
    tokens = bsz * seq
    assert w_in.shape[2] == 6 * width + nh and w_xq.shape[2] == N_XHEADS * HEAD_DIM
    assert seq % MAX_WINDOW == 0

    cos, sa, sb = _rope_tables(positions)
    xf = x.reshape(tokens, d_model)
    xb = xf.astype(BF16)
    memb = mem.reshape(-1, d_model).astype(BF16)

    for l in range(depth):
        w_main = jnp.concatenate([w_in[l, :, :3 * width], w_in[l, :, 3 * width + nh:]], axis=1).astype(BF16)
        w_ft = w_in[l, :, 3 * width:3 * width + nh].T.astype(BF16)

        qkv = _in_projection(xb, w_main, cos, sa, sb, width)
        c = _forget_cumsum(xb, w_ft, b_forget[l], bsz, seq)
        y_fox = _fox_attention(qkv, c, g_fox[l], bsz, seq, nh)
        y_dil = _dilated_attention(qkv, g_dil[l], bsz, seq, nh, 3 * nh)
        f = _out_projection(y_fox, y_dil, w_out[l].astype(BF16))
        xf, xb = _residual_ln(xf, f, ln1_g[l], ln1_b[l], alpha)

        w_kv = jnp.concatenate([w_xk[l], w_xv[l]], axis=1).astype(BF16)
        kv = _matmul(memb, w_kv, BF16, name="memory_kv")
        xf, xb = _cross_attention(xb, xf, kv, w_xq[l].astype(BF16), w_xo[l].astype(BF16),
                                  ln2_g[l], ln2_b[l], alpha, seq)

        hidden = _matmul(xb, w_up[l].astype(BF16), BF16, activation="relu2", name="mlp_up")
        f = _matmul_kacc(hidden, w_down[l].astype(BF16), name="mlp_down")
        xf, xb = _residual_ln(xf, f, ln3_g[l], ln3_b[l], alpha)

    return xf.reshape(bsz, seq, d_model)
```

```python
import functools
import math

import jax
import jax.numpy as jnp
import numpy as np
from jax import lax
from jax.experimental import pallas as pl
from jax.experimental.pallas import tpu as pltpu

F32 = jnp.float32
BF16 = jnp.bfloat16

HEAD_DIM = 128
ROPE_DIM = HEAD_DIM // 4
ROPE_HALF = ROPE_DIM // 2
ROPE_THETA = 500000.0
DILATION_CONFIGS = ((128, 1), (512, 4), (2048, 16))
MAX_WINDOW = max(w for w, _ in DILATION_CONFIGS)
N_XHEADS = 4
LN_EPS = 1e-5
NEG_INF = -1e30
ATTN_SCALE = HEAD_DIM ** -0.5
LOG2E = math.log2(math.e)
Q_SCALE_LOG2 = ATTN_SCALE * LOG2E

V7X_VMEM_BYTES = 64 * 2**20
LANES = 128


def _nbytes(shape, dtype):
    return math.prod(shape) * jnp.dtype(dtype).itemsize


def _params(semantics, blocks, temps=0):
    need = 2 * sum(_nbytes(s, d) for s, d in blocks) + temps
    limit = min(need + need // 4 + (2 << 20), V7X_VMEM_BYTES - (4 << 20))
    return pltpu.CompilerParams(dimension_semantics=semantics, vmem_limit_bytes=int(limit))


def _tile(dim, pref):
    t = min(dim, pref)
    while dim % t:
        t //= 2
    return t


def _rope_table_kernel(pos_ref, inv_ref, cos_ref, sa_ref, sb_ref):
    ang = pos_ref[...].astype(F32) * inv_ref[...]
    lane = lax.broadcasted_iota(jnp.int32, ang.shape, 1)
    cos = jnp.cos(ang)
    sin = jnp.sin(ang)
    cos_ref[...] = cos
    sa_ref[...] = jnp.where(lane < ROPE_HALF, -sin, 0.0)
    sb_ref[...] = jnp.where((lane >= ROPE_HALF) & (lane < ROPE_DIM), sin, 0.0)


def _rope_tables(positions):
    t = positions.size
    rows = _tile(t, 2048)
    inv = ROPE_THETA ** (-jnp.arange(0, ROPE_DIM, 2, dtype=F32) / ROPE_DIM)
    inv_lane = jnp.concatenate([inv, inv, jnp.zeros((LANES - ROPE_DIM,), F32)]).reshape(1, LANES)
    out = jax.ShapeDtypeStruct((t, LANES), F32)
    blk = pl.BlockSpec((rows, LANES), lambda i: (i, 0))
    return pl.pallas_call(
        _rope_table_kernel,
        grid=(t // rows,),
        in_specs=[pl.BlockSpec((rows, 1), lambda i: (i, 0)),
                  pl.BlockSpec((1, LANES), lambda i: (0, 0))],
        out_specs=[blk, blk, blk],
        out_shape=[out, out, out],
        compiler_params=_params(("parallel",), [((rows, LANES), F32)] * 4, temps=8 * rows * LANES * 4),
        name="rope_tables",
    )(positions.reshape(t, 1), inv_lane)


def _inproj_kernel(x_ref, w_ref, cos_ref, sa_ref, sb_ref, o_ref, *, tiles_per_seg):
    seg = pl.program_id(1) // tiles_per_seg
    acc = jnp.dot(x_ref[...], w_ref[...], preferred_element_type=F32)
    bn = acc.shape[1]

    @pl.when(seg == 0)
    def _():
        o_ref[...] = (acc * Q_SCALE_LOG2).astype(o_ref.dtype)

    @pl.when((seg == 1) | (seg == 2) | (seg == 5))
    def _():
        o_ref[...] = acc.astype(o_ref.dtype)

    @pl.when((seg == 3) | (seg == 4))
    def _():
        mult = jnp.where(seg == 3, Q_SCALE_LOG2, 1.0).astype(F32)
        cos = cos_ref[...] * mult
        sa = sa_ref[...] * mult
        sb = sb_ref[...] * mult
        for c in range(bn // HEAD_DIM):
            blk = acc[:, c * HEAD_DIM:(c + 1) * HEAD_DIM]
            rot = (blk * cos
                   + pltpu.roll(blk, HEAD_DIM - ROPE_HALF, 1) * sa
                   + pltpu.roll(blk, ROPE_HALF, 1) * sb)
            o_ref[:, c * HEAD_DIM:(c + 1) * HEAD_DIM] = rot.astype(o_ref.dtype)


def _in_projection(xb, w, cos, sa, sb, seg_width):
    m, k = xb.shape
    n = w.shape[1]
    bm = _tile(m, 1024)
    bn = _tile(seg_width, 1024)
    tab = pl.BlockSpec((bm, LANES), lambda i, j: (i, 0))
    return pl.pallas_call(
        functools.partial(_inproj_kernel, tiles_per_seg=seg_width // bn),
        grid=(m // bm, n // bn),
        in_specs=[pl.BlockSpec((bm, k), lambda i, j: (i, 0)),
                  pl.BlockSpec((k, bn), lambda i, j: (0, j)),
                  tab, tab, tab],
        out_specs=pl.BlockSpec((bm, bn), lambda i, j: (i, j)),
        out_shape=jax.ShapeDtypeStruct((m, n), BF16),
        compiler_params=_params(
            ("parallel", "arbitrary"),
            [((bm, k), BF16), ((k, bn), BF16), ((bm, bn), BF16)] + [((bm, LANES), F32)] * 3,
            temps=2 * bm * bn * 4),
        name="in_projection",
    )(xb, w, cos, sa, sb)


def _forget_kernel(x_ref, w_ref, b_ref, c_ref, carry_ref):
    @pl.when(pl.program_id(1) == 0)
    def _():
        carry_ref[...] = jnp.zeros_like(carry_ref)

    z = lax.dot_general(w_ref[...], x_ref[...], (((1,), (1,)), ((), ())),
                        preferred_element_type=F32) + b_ref[...]
    log_f = jnp.minimum(z, 0.0) - jnp.log1p(jnp.exp(-jnp.abs(z)))
    ts = z.shape[1]
    row = lax.broadcasted_iota(jnp.int32, (ts, ts), 0)
    col = lax.broadcasted_iota(jnp.int32, (ts, ts), 1)
    upper = (row <= col).astype(BF16)
    hi = log_f.astype(BF16)
    r1 = log_f - hi.astype(F32)
    mid = r1.astype(BF16)
    lo = (r1 - mid.astype(F32)).astype(BF16)
    cum = (jnp.dot(hi, upper, preferred_element_type=F32)
           + jnp.dot(mid, upper, preferred_element_type=F32)
           + jnp.dot(lo, upper, preferred_element_type=F32))
    c = cum + carry_ref[:, 0:1]
    c_ref[...] = c * LOG2E
    carry_ref[...] = jnp.broadcast_to(c[:, ts - 1:ts], carry_ref.shape)


def _forget_cumsum(xb, w_ft, b_f, bsz, seq):
    nh, k = w_ft.shape
    ts = _tile(seq, 512)
    ns = seq // ts
    return pl.pallas_call(
        _forget_kernel,
        grid=(bsz, ns),
        in_specs=[pl.BlockSpec((ts, k), lambda b, s: (b * ns + s, 0)),
                  pl.BlockSpec((nh, k), lambda b, s: (0, 0)),
                  pl.BlockSpec((nh, 1), lambda b, s: (0, 0))],
        out_specs=pl.BlockSpec((None, nh, ts), lambda b, s: (b, 0, s)),
        out_shape=jax.ShapeDtypeStruct((bsz, nh, seq), F32),
        scratch_shapes=[pltpu.VMEM((nh, LANES), F32)],
        compiler_params=_params(
            ("parallel", "arbitrary"),
            [((ts, k), BF16), ((nh, k), BF16), ((nh, ts), F32)],
            temps=4 * ts * ts * 4),
        name="forget_cumsum",
    )(xb, w_ft, b_f.reshape(nh, 1))


ATTN_CHUNK = 128


ATTN_HEADS = 4
ATTN_TILE = 512


def _head_cols(g):
    return slice(g * HEAD_DIM, (g + 1) * HEAD_DIM)


def _flash_tiles(q_ref, k_ref, v_ref, scratch, n_tiles, key_start, bias, mask_last):
    s_ref, mx_ref, m_ref, accl_ref = scratch
    tq = q_ref.shape[0]
    heads = q_ref.shape[1] // HEAD_DIM
    units = [(g, slice(r, r + ATTN_CHUNK)) for g in range(heads) for r in range(0, tq, ATTN_CHUNK)]
    m_ref[...] = jnp.full_like(m_ref, NEG_INF)
    accl_ref[...] = jnp.zeros_like(accl_ref)

    def keys(t):
        return k_ref[pl.ds(key_start(t), tq), :]

    def values(t):
        v = v_ref[pl.ds(key_start(t), tq), :]
        ones = jnp.ones((tq, HEAD_DIM), v.dtype)
        return [jnp.concatenate([v[:, _head_cols(g)], ones], axis=1) for g in range(heads)]

    def produce(t, k, g, rows, masked):
        s = lax.dot_general(q_ref[rows, _head_cols(g)], k[:, _head_cols(g)],
                            (((1,), (1,)), ((), ())), preferred_element_type=F32) + bias(t, g, rows, masked)
        s_ref[g, rows, :] = s
        mx_ref[g, rows, :] = jnp.broadcast_to(jnp.max(s, axis=1, keepdims=True), (ATTN_CHUNK, LANES))

    def consume(v_aug, g, rows):
        m_prev = m_ref[g, rows, :]
        m_next = jnp.maximum(m_prev, mx_ref[g, rows, :])
        p = jnp.exp2(s_ref[g, rows, :] - jnp.tile(m_next, (1, tq // LANES)))
        alpha = jnp.exp2(m_prev - m_next)
        pv = jnp.dot(p.astype(v_aug[g].dtype), v_aug[g], preferred_element_type=F32)
        accl_ref[g, rows, :] = jnp.tile(alpha, (1, 2)) * accl_ref[g, rows, :] + pv
        m_ref[g, rows, :] = m_next

    def produce_tile(t, masked):
        k = keys(t)
        for g, rows in units:
            produce(t, k, g, rows, masked)

    def advance(t, masked):
        v_aug = values(t)
        k = keys(t + 1)
        for g, rows in units:
            consume(v_aug, g, rows)
            produce(t + 1, k, g, rows, masked)

    def body(t, carry):
        advance(t, False)
        return carry

    if mask_last:
        pl.when(n_tiles > 1)(lambda: produce_tile(0, False))
        pl.when(n_tiles == 1)(lambda: produce_tile(0, True))
        lax.fori_loop(0, n_tiles - 2, body, 0)
        pl.when(n_tiles > 1)(lambda: advance(n_tiles - 2, True))
    else:
        produce_tile(0, False)
        lax.fori_loop(0, n_tiles - 1, body, 0)
    v_aug = values(n_tiles - 1)
    for g, rows in units:
        consume(v_aug, g, rows)


def _finish_heads(accl_ref, g_ref, o_ref):
    for g in range(accl_ref.shape[0]):
        o = accl_ref[g, :, :HEAD_DIM] / accl_ref[g, :, HEAD_DIM:]
        y = o * lax.rsqrt(jnp.mean(o * o, axis=1, keepdims=True) + LN_EPS)
        o_ref[:, _head_cols(g)] = (y * g_ref[:, _head_cols(g)]).astype(o_ref.dtype)


def _attention_call(kernel_fn, qkv, extra, extra_spec, g, bsz, seq, nh, col0, extra_bytes, name):
    tq = _tile(seq, ATTN_TILE)
    nq = seq // tq
    heads = math.gcd(ATTN_HEADS, nh)
    wide = heads * HEAD_DIM
    groups = nh // heads
    first = col0 // heads
    assert col0 % heads == 0 and tq % ATTN_CHUNK == 0
    return pl.pallas_call(
        kernel_fn,
        grid=(bsz, groups, nq),
        in_specs=[pl.BlockSpec((tq, wide), lambda b, h, i: (b * nq + i, first + h)),
                  pl.BlockSpec((seq, wide), lambda b, h, i: (b, first + groups + h)),
                  pl.BlockSpec((seq, wide), lambda b, h, i: (b, first + 2 * groups + h)),
                  extra_spec(heads),
                  pl.BlockSpec((1, wide), lambda b, h, i: (0, h))],
        out_specs=pl.BlockSpec((tq, wide), lambda b, h, i: (b * nq + i, h)),
        out_shape=jax.ShapeDtypeStruct((bsz * seq, nh * HEAD_DIM), BF16),
        scratch_shapes=[pltpu.VMEM((heads, tq, tq), F32),
                        pltpu.VMEM((heads, tq, LANES), F32),
                        pltpu.VMEM((heads, tq, LANES), F32),
                        pltpu.VMEM((heads, tq, 2 * HEAD_DIM), F32)],
        compiler_params=_params(
            ("parallel", "parallel", "arbitrary"),
            [((tq, wide), BF16)] * 2 + [((seq, wide), BF16)] * 2,
            temps=2 * extra_bytes + heads * 10 * tq * tq * 4),
        name=name,
    )(qkv, qkv, qkv, extra, g.reshape(1, nh * HEAD_DIM))


def _fox_kernel(q_ref, k_ref, v_ref, c_ref, g_ref, o_ref, *scratch):
    qi = pl.program_id(2)
    tq = q_ref.shape[0]

    def key_start(t):
        return pl.multiple_of(t * tq, tq)

    def bias(t, g, rows, masked):
        decay = -c_ref[g, :, pl.ds(key_start(t), tq)]
        if not masked:
            return decay
        row = rows.start + lax.broadcasted_iota(jnp.int32, (ATTN_CHUNK, tq), 0)
        col = lax.broadcasted_iota(jnp.int32, (ATTN_CHUNK, tq), 1)
        return jnp.where(col <= row, decay, NEG_INF)

    _flash_tiles(q_ref, k_ref, v_ref, scratch, qi + 1, key_start, bias, mask_last=True)
    _finish_heads(scratch[-1], g_ref, o_ref)


def _fox_attention(qkv, c, g, bsz, seq, nh):
    return _attention_call(
        _fox_kernel, qkv, c.reshape(bsz, nh, 1, seq),
        lambda heads: pl.BlockSpec((None, heads, 1, seq), lambda b, h, i: (b, h, 0, 0)),
        g, bsz, seq, nh, 0, extra_bytes=ATTN_HEADS * 8 * seq * 4, name="fox_attention")


def _dilated_bias_tables(tq):
    n_off = MAX_WINDOW // tq + 1
    off = np.arange(n_off)[:, None, None] * tq
    dist = off + np.arange(tq)[None, :, None] - np.arange(tq)[None, None, :]
    count = np.zeros(dist.shape, np.float64)
    for window, dilation in DILATION_CONFIGS:
        count += (dist >= 0) & (dist <= window) & (dist % dilation == 0)
    return np.where(count > 0, np.log2(np.maximum(count, 1.0)), NEG_INF).astype(np.float32)


def _dilated_kernel(q_ref, k_ref, v_ref, bias_ref, g_ref, o_ref, *scratch):
    qi = pl.program_id(2)
    tq = q_ref.shape[0]
    n_tiles = jnp.minimum(qi + 1, bias_ref.shape[0])

    def key_start(t):
        return pl.multiple_of((qi - t) * tq, tq)

    def bias(t, g, rows, masked):
        return bias_ref[t, rows, :]

    _flash_tiles(q_ref, k_ref, v_ref, scratch, n_tiles, key_start, bias, mask_last=False)
    _finish_heads(scratch[-1], g_ref, o_ref)


def _dilated_attention(qkv, g, bsz, seq, nh, col0):
    tq = _tile(seq, ATTN_TILE)
    assert MAX_WINDOW % tq == 0
    bias = jnp.asarray(_dilated_bias_tables(tq))
    return _attention_call(
        _dilated_kernel, qkv, bias,
        lambda heads: pl.BlockSpec(bias.shape, lambda b, h, i: (0, 0, 0)),
        g, bsz, seq, nh, col0, extra_bytes=bias.size * 4, name="dilated_attention")


def _outproj_kernel(a_ref, b_ref, wa_ref, wb_ref, o_ref):
    o_ref[...] = (jnp.dot(a_ref[...], wa_ref[...], preferred_element_type=F32)
                  + jnp.dot(b_ref[...], wb_ref[...], preferred_element_type=F32))


def _out_projection(ya, yb, w):
    m, ka = ya.shape
    kb = yb.shape[1]
    assert ka == kb and w.shape[0] == ka + kb
    n = w.shape[1]
    bm = _tile(m, 1024)
    bn = _tile(n, 1024)
    return pl.pallas_call(
        _outproj_kernel,
        grid=(m // bm, n // bn),
        in_specs=[pl.BlockSpec((bm, ka), lambda i, j: (i, 0)),
                  pl.BlockSpec((bm, kb), lambda i, j: (i, 0)),
                  pl.BlockSpec((ka, bn), lambda i, j: (0, j)),
                  pl.BlockSpec((kb, bn), lambda i, j: (1, j))],
        out_specs=pl.BlockSpec((bm, bn), lambda i, j: (i, j)),
        out_shape=jax.ShapeDtypeStruct((m, n), F32),
        compiler_params=_params(
            ("parallel", "arbitrary"),
            [((bm, ka), BF16)] * 2 + [((ka, bn), BF16)] * 2 + [((bm, bn), F32)],
            temps=bm * bn * 4),
        name="out_projection",
    )(ya, yb, w, w)


def _matmul_kernel(x_ref, w_ref, o_ref, *, activation):
    acc = jnp.dot(x_ref[...], w_ref[...], preferred_element_type=F32)
    if activation == "relu2":
        r = jnp.maximum(acc, 0.0)
        acc = r * r
    o_ref[...] = acc.astype(o_ref.dtype)


def _matmul(x, w, out_dtype, activation=None, name="matmul"):
    m, k = x.shape
    n = w.shape[1]
    bm = _tile(m, 1024)
    bn = _tile(n, 1024)
    return pl.pallas_call(
        functools.partial(_matmul_kernel, activation=activation),
        grid=(m // bm, n // bn),
        in_specs=[pl.BlockSpec((bm, k), lambda i, j: (i, 0)),
                  pl.BlockSpec((k, bn), lambda i, j: (0, j))],
        out_specs=pl.BlockSpec((bm, bn), lambda i, j: (i, j)),
        out_shape=jax.ShapeDtypeStruct((m, n), out_dtype),
        compiler_params=_params(
            ("parallel", "arbitrary"),
            [((bm, k), BF16), ((k, bn), BF16), ((bm, bn), out_dtype)],
            temps=2 * bm * bn * 4),
        name=name,
    )(x, w)


def _matmul_kacc_kernel(x_ref, w_ref, o_ref):
    @pl.when(pl.program_id(2) == 0)
    def _():
        o_ref[...] = jnp.zeros_like(o_ref)

    o_ref[...] += jnp.dot(x_ref[...], w_ref[...], preferred_element_type=F32)


def _matmul_kacc(x, w, name):
    m, k = x.shape
    n = w.shape[1]
    bm = _tile(m, 1024)
    bn = _tile(n, 1024)
    bk = _tile(k, 2048)
    return pl.pallas_call(
        _matmul_kacc_kernel,
        grid=(m // bm, n // bn, k // bk),
        in_specs=[pl.BlockSpec((bm, bk), lambda i, j, kk: (i, kk)),
                  pl.BlockSpec((bk, bn), lambda i, j, kk: (kk, j))],
        out_specs=pl.BlockSpec((bm, bn), lambda i, j, kk: (i, j)),
        out_shape=jax.ShapeDtypeStruct((m, n), F32),
        compiler_params=_params(
            ("parallel", "parallel", "arbitrary"),
            [((bm, bk), BF16), ((bk, bn), BF16), ((bm, bn), F32)],
            temps=bm * bn * 4),
        name=name,
    )(x, w)


def _layer_norm_rows(z, g, b):
    mu = jnp.mean(z, axis=1, keepdims=True)
    d = z - mu
    var = jnp.mean(d * d, axis=1, keepdims=True)
    return d * lax.rsqrt(var + LN_EPS) * g + b


def _residual_ln_kernel(x_ref, f_ref, g_ref, b_ref, o_ref, ob_ref, *, alpha):
    y = _layer_norm_rows(alpha * x_ref[...] + f_ref[...], g_ref[...], b_ref[...])
    o_ref[...] = y
    ob_ref[...] = y.astype(ob_ref.dtype)


def _residual_ln(x, f, g, b, alpha):
    m, d = x.shape
    bm = _tile(m, 256)
    row = pl.BlockSpec((bm, d), lambda i: (i, 0))
    vec = pl.BlockSpec((1, d), lambda i: (0, 0))
    return pl.pallas_call(
        functools.partial(_residual_ln_kernel, alpha=alpha),
        grid=(m // bm,),
        in_specs=[row, row, vec, vec],
        out_specs=[row, row],
        out_shape=[jax.ShapeDtypeStruct((m, d), F32), jax.ShapeDtypeStruct((m, d), BF16)],
        compiler_params=_params(("parallel",), [((bm, d), F32)] * 3 + [((bm, d), BF16)],
                                temps=3 * bm * d * 4),
        name="residual_layer_norm",
    )(x, f, g.reshape(1, d), b.reshape(1, d))


def _cross_kernel(xb_ref, x_ref, kv_ref, wq_ref, wo_ref, g_ref, b_ref, o_ref, ob_ref, *, alpha):
    q = (jnp.dot(xb_ref[...], wq_ref[...], preferred_element_type=F32) * ATTN_SCALE).astype(BF16)
    width = N_XHEADS * HEAD_DIM
    heads = []
    for h in range(N_XHEADS):
        lo, hi = h * HEAD_DIM, (h + 1) * HEAD_DIM
        s = lax.dot_general(q[:, lo:hi], kv_ref[:, lo:hi], (((1,), (1,)), ((), ())),
                            preferred_element_type=F32)
        p = jnp.exp(s - jnp.max(s, axis=1, keepdims=True))
        o = jnp.dot(p.astype(BF16), kv_ref[:, width + lo:width + hi], preferred_element_type=F32)
        heads.append((o / jnp.sum(p, axis=1, keepdims=True)).astype(BF16))
    f = jnp.dot(jnp.concatenate(heads, axis=1), wo_ref[...], preferred_element_type=F32)
    y = _layer_norm_rows(alpha * x_ref[...] + f, g_ref[...], b_ref[...])
    o_ref[...] = y
    ob_ref[...] = y.astype(ob_ref.dtype)


def _cross_attention(xb, x, kv, wq, wo, g, b, alpha, seq):
    m, d = x.shape
    n_mem = kv.shape[0] // (m // seq)
    width = N_XHEADS * HEAD_DIM
    bm = _tile(seq, 256)
    per_batch = seq // bm
    row_b = pl.BlockSpec((bm, d), lambda i: (i, 0))
    vec = pl.BlockSpec((1, d), lambda i: (0, 0))
    return pl.pallas_call(
        functools.partial(_cross_kernel, alpha=alpha),
        grid=(m // bm,),
        in_specs=[row_b, row_b,
                  pl.BlockSpec((n_mem, 2 * width), lambda i: (i // per_batch, 0)),
                  pl.BlockSpec((d, width), lambda i: (0, 0)),
                  pl.BlockSpec((width, d), lambda i: (0, 0)),
                  vec, vec],
        out_specs=[row_b, row_b],
        out_shape=[jax.ShapeDtypeStruct((m, d), F32), jax.ShapeDtypeStruct((m, d), BF16)],
        compiler_params=_params(
            ("parallel",),
            [((bm, d), BF16)] * 2 + [((bm, d), F32)] * 2
            + [((n_mem, 2 * width), BF16), ((d, width), BF16), ((width, d), BF16)],
            temps=4 * bm * d * 4),
        name="cross_attention",
    )(xb, x, kv, wq, wo, g.reshape(1, d), b.reshape(1, d))


def kernel(x, mem, positions, w_in, b_forget, g_fox, g_dil, w_out, ln1_g, ln1_b,
           w_xq, w_xk, w_xv, w_xo, ln2_g, ln2_b, w_up, w_down, ln3_g, ln3_b):
    bsz, seq, d_model = x.shape
    depth = w_in.shape[0]
    nh = d_model // (2 * HEAD_DIM)
    width = nh * HEAD_DIM
    alpha = (2 * depth) ** 0.25
    tokens = bsz * seq
    assert w_in.shape[2] == 6 * width + nh and w_xq.shape[2] == N_XHEADS * HEAD_DIM
    assert seq % MAX_WINDOW == 0

    cos, sa, sb = _rope_tables(positions)
    xf = x.reshape(tokens, d_model)
    xb = xf.astype(BF16)
    memb = mem.reshape(-1, d_model).astype(BF16)

    for l in range(depth):
        w_main = jnp.concatenate([w_in[l, :, :3 * width], w_in[l, :, 3 * width + nh:]], axis=1).astype(BF16)
        w_ft = w_in[l, :, 3 * width:3 * width + nh].T.astype(BF16)

        qkv = _in_projection(xb, w_main, cos, sa, sb, width)
        c = _forget_cumsum(xb, w_ft, b_forget[l], bsz, seq)
        y_fox = _fox_attention(qkv, c, g_fox[l], bsz, seq, nh)
        y_dil = _dilated_attention(qkv, g_dil[l], bsz, seq, nh, 3 * nh)
        f = _out_projection(y_fox, y_dil, w_out[l].astype(BF16))
        xf, xb = _residual_ln(xf, f, ln1_g[l], ln1_b[l], alpha)

        w_kv = jnp.concatenate([w_xk[l], w_xv[l]], axis=1).astype(BF16)
        kv = _matmul(memb, w_kv, BF16, name="memory_kv")
        xf, xb = _cross_attention(xb, xf, kv, w_xq[l].astype(BF16), w_xo[l].astype(BF16),
                                  ln2_g[l], ln2_b[l], alpha, seq)

        hidden = _matmul(xb, w_up[l].astype(BF16), BF16, activation="relu2", name="mlp_up")
        f = _matmul_kacc(hidden, w_down[l].astype(BF16), name="mlp_down")
        xf, xb = _residual_ln(xf, f, ln3_g[l], ln3_b[l], alpha)

    return xf.reshape(bsz, seq, d_model)
```

```python
import functools
import math

import jax
import jax.numpy as jnp
import numpy as np
from jax import lax
from jax.experimental import pallas as pl
from jax.experimental.pallas import tpu as pltpu

F32 = jnp.float32
BF16 = jnp.bfloat16

HEAD_DIM = 128
ROPE_DIM = HEAD_DIM // 4
ROPE_HALF = ROPE_DIM // 2
ROPE_THETA = 500000.0
DILATION_CONFIGS = ((128, 1), (512, 4), (2048, 16))
MAX_WINDOW = max(w for w, _ in DILATION_CONFIGS)
N_XHEADS = 4
LN_EPS = 1e-5
NEG_INF = -1e30
ATTN_SCALE = HEAD_DIM ** -0.5
LOG2E = math.log2(math.e)
Q_SCALE_LOG2 = ATTN_SCALE * LOG2E

V7X_VMEM_BYTES = 64 * 2**20
LANES = 128


def _nbytes(shape, dtype):
    return math.prod(shape) * jnp.dtype(dtype).itemsize


def _params(semantics, blocks, temps=0):
    need = 2 * sum(_nbytes(s, d) for s, d in blocks) + temps
    limit = min(need + need // 4 + (2 << 20), V7X_VMEM_BYTES - (4 << 20))
    return pltpu.CompilerParams(dimension_semantics=semantics, vmem_limit_bytes=int(limit))


def _tile(dim, pref):
    t = min(dim, pref)
    while dim % t:
        t //= 2
    return t


SEG_Q_FOX, SEG_K_FOX, SEG_V_FOX, SEG_Q_DIL, SEG_K_DIL, SEG_V_DIL = range(6)


def _rope_table_kernel(pos_ref, inv_ref, cos_ref, sa_ref, sb_ref):
    ang = pos_ref[...].astype(F32) * inv_ref[...]
    lane = lax.broadcasted_iota(jnp.int32, ang.shape, 1)
    cos = jnp.cos(ang)
    sin = jnp.sin(ang)
    sa = jnp.where(lane < ROPE_HALF, -sin, 0.0)
    sb = jnp.where((lane >= ROPE_HALF) & (lane < ROPE_DIM), sin, 0.0)
    for ref, t in ((cos_ref, cos), (sa_ref, sa), (sb_ref, sb)):
        ref[0] = t * Q_SCALE_LOG2
        ref[1] = t


def _rope_tables(positions):
    t = positions.size
    rows = _tile(t, 2048)
    inv = ROPE_THETA ** (-jnp.arange(0, ROPE_DIM, 2, dtype=F32) / ROPE_DIM)
    inv_lane = jnp.concatenate([inv, inv, jnp.zeros((LANES - ROPE_DIM,), F32)]).reshape(1, LANES)
    out = jax.ShapeDtypeStruct((2, t, LANES), F32)
    blk = pl.BlockSpec((2, rows, LANES), lambda i: (0, i, 0))
    return pl.pallas_call(
        _rope_table_kernel,
        grid=(t // rows,),
        in_specs=[pl.BlockSpec((rows, 1), lambda i: (i, 0)),
                  pl.BlockSpec((1, LANES), lambda i: (0, 0))],
        out_specs=[blk, blk, blk],
        out_shape=[out, out, out],
        compiler_params=_params(("parallel",), [((rows, LANES), F32)] + [((2, rows, LANES), F32)] * 3,
                                temps=8 * rows * LANES * 4),
        name="rope_tables",
    )(positions.reshape(t, 1), inv_lane)


def _inproj_kernel(x_ref, w_ref, cos_ref, sa_ref, sb_ref, o_ref, *, tiles_per_seg):
    seg = pl.program_id(1) // tiles_per_seg
    rotary = (seg == SEG_Q_DIL) | (seg == SEG_K_DIL)

    @pl.when(rotary)
    def _():
        acc = jnp.dot(x_ref[...], w_ref[...], preferred_element_type=F32)
        cos, sa, sb = cos_ref[...], sa_ref[...], sb_ref[...]
        for c in range(acc.shape[1] // HEAD_DIM):
            cols = slice(c * HEAD_DIM, (c + 1) * HEAD_DIM)
            blk = acc[:, cols]
            out = (blk * cos + pltpu.roll(blk, HEAD_DIM - ROPE_HALF, 1) * sa
                   + pltpu.roll(blk, ROPE_HALF, 1) * sb)
            o_ref[:, cols] = out.astype(o_ref.dtype)

    @pl.when(jnp.logical_not(rotary))
    def _():
        acc = jnp.dot(x_ref[...], w_ref[...], preferred_element_type=F32)
        scale = jnp.where(seg == SEG_Q_FOX, Q_SCALE_LOG2, 1.0).astype(F32)
        o_ref[...] = (acc * scale).astype(o_ref.dtype)


def _in_projection(xb, w, layer, cos, sa, sb, seg_width):
    m, k = xb.shape
    n = w.shape[2]
    bm = _tile(m, 1024)
    bn = _tile(seg_width, 1024)
    tiles_per_seg = seg_width // bn

    def table_set(j):
        return jnp.where(j // tiles_per_seg == SEG_Q_DIL, 0, 1)

    tab = pl.BlockSpec((None, bm, LANES), lambda i, j: (table_set(j), i, 0))
    return pl.pallas_call(
        functools.partial(_inproj_kernel, tiles_per_seg=tiles_per_seg),
        grid=(m // bm, n // bn),
        in_specs=[pl.BlockSpec((bm, k), lambda i, j: (i, 0)),
                  pl.BlockSpec((None, k, bn), lambda i, j: (layer, 0, j)),
                  tab, tab, tab],
        out_specs=pl.BlockSpec((bm, bn), lambda i, j: (i, j)),
        out_shape=jax.ShapeDtypeStruct((m, n), BF16),
        compiler_params=_params(
            ("parallel", "arbitrary"),
            [((bm, k), BF16), ((k, bn), BF16), ((bm, bn), BF16)] + [((bm, LANES), F32)] * 3,
            temps=2 * bm * bn * 4),
        name="in_projection",
    )(xb, w, cos, sa, sb)


def _forget_kernel(x_ref, w_ref, b_ref, c_ref, carry_ref):
    @pl.when(pl.program_id(1) == 0)
    def _():
        carry_ref[...] = jnp.zeros_like(carry_ref)

    z = lax.dot_general(w_ref[...], x_ref[...], (((1,), (1,)), ((), ())),
                        preferred_element_type=F32) + b_ref[...]
    log_f = jnp.minimum(z, 0.0) - jnp.log1p(jnp.exp(-jnp.abs(z)))
    ts = z.shape[1]
    row = lax.broadcasted_iota(jnp.int32, (ts, ts), 0)
    col = lax.broadcasted_iota(jnp.int32, (ts, ts), 1)
    upper = (row <= col).astype(BF16)
    hi = log_f.astype(BF16)
    r1 = log_f - hi.astype(F32)
    mid = r1.astype(BF16)
    lo = (r1 - mid.astype(F32)).astype(BF16)
    cum = (jnp.dot(hi, upper, preferred_element_type=F32)
           + jnp.dot(mid, upper, preferred_element_type=F32)
           + jnp.dot(lo, upper, preferred_element_type=F32))
    c = cum + carry_ref[:, 0:1]
    c_ref[...] = c * LOG2E
    carry_ref[...] = jnp.broadcast_to(c[:, ts - 1:ts], carry_ref.shape)


def _forget_cumsum(xb, w_ft, layer, b_f, bsz, seq):
    _, nh, k = w_ft.shape
    ts = _tile(seq, 512)
    ns = seq // ts
    return pl.pallas_call(
        _forget_kernel,
        grid=(bsz, ns),
        in_specs=[pl.BlockSpec((ts, k), lambda b, s: (b * ns + s, 0)),
                  pl.BlockSpec((None, nh, k), lambda b, s: (layer, 0, 0)),
                  pl.BlockSpec((nh, 1), lambda b, s: (0, 0))],
        out_specs=pl.BlockSpec((None, nh, ts), lambda b, s: (b, 0, s)),
        out_shape=jax.ShapeDtypeStruct((bsz, nh, seq), F32),
        scratch_shapes=[pltpu.VMEM((nh, LANES), F32)],
        compiler_params=_params(
            ("parallel", "arbitrary"),
            [((ts, k), BF16), ((nh, k), BF16), ((nh, ts), F32)],
            temps=4 * ts * ts * 4),
        name="forget_cumsum",
    )(xb, w_ft, b_f.reshape(nh, 1))


ATTN_CHUNK = 128


ATTN_HEADS = 4
ATTN_TILE = 512


def _head_cols(g):
    return slice(g * HEAD_DIM, (g + 1) * HEAD_DIM)


def _flash_tiles(q_ref, k_ref, v_ref, scratch, n_tiles, key_start, bias, mask_last):
    s_ref, mx_ref, m_ref, accl_ref = scratch
    tq = q_ref.shape[0]
    heads = q_ref.shape[1] // HEAD_DIM
    units = [(g, slice(r, r + ATTN_CHUNK)) for g in range(heads) for r in range(0, tq, ATTN_CHUNK)]
    m_ref[...] = jnp.full_like(m_ref, NEG_INF)
    accl_ref[...] = jnp.zeros_like(accl_ref)

    def keys(t):
        return k_ref[pl.ds(key_start(t), tq), :]

    def values(t):
        v = v_ref[pl.ds(key_start(t), tq), :]
        ones = jnp.ones((tq, HEAD_DIM), v.dtype)
        return [jnp.concatenate([v[:, _head_cols(g)], ones], axis=1) for g in range(heads)]

    def produce(t, k, g, rows, masked):
        s = lax.dot_general(q_ref[rows, _head_cols(g)], k[:, _head_cols(g)],
                            (((1,), (1,)), ((), ())), preferred_element_type=F32) + bias(t, g, rows, masked)
        s_ref[g, rows, :] = s
        mx_ref[g, rows, :] = jnp.broadcast_to(jnp.max(s, axis=1, keepdims=True), (ATTN_CHUNK, LANES))

    def consume(v_aug, g, rows):
        m_prev = m_ref[g, rows, :]
        m_next = jnp.maximum(m_prev, mx_ref[g, rows, :])
        p = jnp.exp2(s_ref[g, rows, :] - jnp.tile(m_next, (1, tq // LANES)))
        alpha = jnp.exp2(m_prev - m_next)
        pv = jnp.dot(p.astype(v_aug[g].dtype), v_aug[g], preferred_element_type=F32)
        accl_ref[g, rows, :] = jnp.tile(alpha, (1, 2)) * accl_ref[g, rows, :] + pv
        m_ref[g, rows, :] = m_next

    def produce_tile(t, masked):
        k = keys(t)
        for g, rows in units:
            produce(t, k, g, rows, masked)

    def advance(t, masked):
        v_aug = values(t)
        k = keys(t + 1)
        for g, rows in units:
            consume(v_aug, g, rows)
            produce(t + 1, k, g, rows, masked)

    def body(t, carry):
        advance(t, False)
        return carry

    if mask_last:
        pl.when(n_tiles > 1)(lambda: produce_tile(0, False))
        pl.when(n_tiles == 1)(lambda: produce_tile(0, True))
        lax.fori_loop(0, n_tiles - 2, body, 0)
        pl.when(n_tiles > 1)(lambda: advance(n_tiles - 2, True))
    else:
        produce_tile(0, False)
        lax.fori_loop(0, n_tiles - 1, body, 0)
    v_aug = values(n_tiles - 1)
    for g, rows in units:
        consume(v_aug, g, rows)


def _finish_heads(accl_ref, g_ref, o_ref):
    for g in range(accl_ref.shape[0]):
        o = accl_ref[g, :, :HEAD_DIM] / accl_ref[g, :, HEAD_DIM:]
        y = o * lax.rsqrt(jnp.mean(o * o, axis=1, keepdims=True) + LN_EPS)
        o_ref[:, _head_cols(g)] = (y * g_ref[:, _head_cols(g)]).astype(o_ref.dtype)


def _attention_call(kernel_fn, qkv, extra, extra_spec, g, bsz, seq, nh, col0, extra_bytes, name):
    tq = _tile(seq, ATTN_TILE)
    nq = seq // tq
    heads = math.gcd(ATTN_HEADS, nh)
    wide = heads * HEAD_DIM
    groups = nh // heads
    first = col0 // heads
    assert col0 % heads == 0 and tq % ATTN_CHUNK == 0
    return pl.pallas_call(
        kernel_fn,
        grid=(bsz, groups, nq),
        in_specs=[pl.BlockSpec((tq, wide), lambda b, h, i: (b * nq + i, first + h)),
                  pl.BlockSpec((seq, wide), lambda b, h, i: (b, first + groups + h)),
                  pl.BlockSpec((seq, wide), lambda b, h, i: (b, first + 2 * groups + h)),
                  extra_spec(heads),
                  pl.BlockSpec((1, wide), lambda b, h, i: (0, h))],
        out_specs=pl.BlockSpec((tq, wide), lambda b, h, i: (b * nq + i, h)),
        out_shape=jax.ShapeDtypeStruct((bsz * seq, nh * HEAD_DIM), BF16),
        scratch_shapes=[pltpu.VMEM((heads, tq, tq), F32),
                        pltpu.VMEM((heads, tq, LANES), F32),
                        pltpu.VMEM((heads, tq, LANES), F32),
                        pltpu.VMEM((heads, tq, 2 * HEAD_DIM), F32)],
        compiler_params=_params(
            ("parallel", "parallel", "arbitrary"),
            [((tq, wide), BF16)] * 2 + [((seq, wide), BF16)] * 2,
            temps=2 * extra_bytes + heads * 10 * tq * tq * 4),
        name=name,
    )(qkv, qkv, qkv, extra, g.reshape(1, nh * HEAD_DIM))


def _fox_kernel(q_ref, k_ref, v_ref, c_ref, g_ref, o_ref, *scratch):
    qi = pl.program_id(2)
    tq = q_ref.shape[0]

    def key_start(t):
        return pl.multiple_of(t * tq, tq)

    def bias(t, g, rows, masked):
        decay = -c_ref[g, :, pl.ds(key_start(t), tq)]
        if not masked:
            return decay
        row = rows.start + lax.broadcasted_iota(jnp.int32, (ATTN_CHUNK, tq), 0)
        col = lax.broadcasted_iota(jnp.int32, (ATTN_CHUNK, tq), 1)
        return jnp.where(col <= row, decay, NEG_INF)

    _flash_tiles(q_ref, k_ref, v_ref, scratch, qi + 1, key_start, bias, mask_last=True)
    _finish_heads(scratch[-1], g_ref, o_ref)


def _fox_attention(qkv, c, g, bsz, seq, nh):
    return _attention_call(
        _fox_kernel, qkv, c.reshape(bsz, nh, 1, seq),
        lambda heads: pl.BlockSpec((None, heads, 1, seq), lambda b, h, i: (b, h, 0, 0)),
        g, bsz, seq, nh, 0, extra_bytes=ATTN_HEADS * 8 * seq * 4, name="fox_attention")


def _dilated_bias_tables(tq):
    n_off = MAX_WINDOW // tq + 1
    off = np.arange(n_off)[:, None, None] * tq
    dist = off + np.arange(tq)[None, :, None] - np.arange(tq)[None, None, :]
    count = np.zeros(dist.shape, np.float64)
    for window, dilation in DILATION_CONFIGS:
        count += (dist >= 0) & (dist <= window) & (dist % dilation == 0)
    return np.where(count > 0, np.log2(np.maximum(count, 1.0)), NEG_INF).astype(np.float32)


def _dilated_kernel(q_ref, k_ref, v_ref, bias_ref, g_ref, o_ref, *scratch):
    qi = pl.program_id(2)
    tq = q_ref.shape[0]
    n_tiles = jnp.minimum(qi + 1, bias_ref.shape[0])

    def key_start(t):
        return pl.multiple_of((qi - t) * tq, tq)

    def bias(t, g, rows, masked):
        return bias_ref[t, rows, :]

    _flash_tiles(q_ref, k_ref, v_ref, scratch, n_tiles, key_start, bias, mask_last=False)
    _finish_heads(scratch[-1], g_ref, o_ref)


def _dilated_attention(qkv, g, bsz, seq, nh, col0):
    tq = _tile(seq, ATTN_TILE)
    assert MAX_WINDOW % tq == 0
    bias = jnp.asarray(_dilated_bias_tables(tq))
    return _attention_call(
        _dilated_kernel, qkv, bias,
        lambda heads: pl.BlockSpec(bias.shape, lambda b, h, i: (0, 0, 0)),
        g, bsz, seq, nh, col0, extra_bytes=bias.size * 4, name="dilated_attention")


def _outproj_kernel(a_ref, b_ref, wa_ref, wb_ref, x_ref, o_ref, *, alpha):
    o_ref[...] = (alpha * x_ref[...]
                  + jnp.dot(a_ref[...], wa_ref[...], preferred_element_type=F32)
                  + jnp.dot(b_ref[...], wb_ref[...], preferred_element_type=F32))


def _out_projection(ya, yb, w, layer, x, alpha):
    m, ka = ya.shape
    kb = yb.shape[1]
    assert ka == kb and w.shape[1] == ka + kb
    n = w.shape[2]
    bm = _tile(m, 1024)
    bn = _tile(n, 1024)
    return pl.pallas_call(
        functools.partial(_outproj_kernel, alpha=alpha),
        grid=(m // bm, n // bn),
        in_specs=[pl.BlockSpec((bm, ka), lambda i, j: (i, 0)),
                  pl.BlockSpec((bm, kb), lambda i, j: (i, 0)),
                  pl.BlockSpec((None, ka, bn), lambda i, j: (layer, 0, j)),
                  pl.BlockSpec((None, kb, bn), lambda i, j: (layer, 1, j)),
                  pl.BlockSpec((bm, bn), lambda i, j: (i, j))],
        out_specs=pl.BlockSpec((bm, bn), lambda i, j: (i, j)),
        out_shape=jax.ShapeDtypeStruct((m, n), F32),
        compiler_params=_params(
            ("parallel", "arbitrary"),
            [((bm, ka), BF16)] * 2 + [((ka, bn), BF16)] * 2 + [((bm, bn), F32)] * 2,
            temps=bm * bn * 4),
        name="out_projection",
    )(ya, yb, w, w, x)


def _matmul_kernel(x_ref, w_ref, o_ref, *, activation):
    acc = jnp.dot(x_ref[...], w_ref[...], preferred_element_type=F32)
    if activation == "relu2":
        r = jnp.maximum(acc, 0.0)
        acc = r * r
    o_ref[...] = acc.astype(o_ref.dtype)


def _matmul(x, w, layer, out_dtype, activation=None, name="matmul"):
    m, k = x.shape
    n = w.shape[2]
    bm = _tile(m, 1024)
    bn = _tile(n, 1024)
    return pl.pallas_call(
        functools.partial(_matmul_kernel, activation=activation),
        grid=(m // bm, n // bn),
        in_specs=[pl.BlockSpec((bm, k), lambda i, j: (i, 0)),
                  pl.BlockSpec((None, k, bn), lambda i, j: (layer, 0, j))],
        out_specs=pl.BlockSpec((bm, bn), lambda i, j: (i, j)),
        out_shape=jax.ShapeDtypeStruct((m, n), out_dtype),
        compiler_params=_params(
            ("parallel", "arbitrary"),
            [((bm, k), BF16), ((k, bn), BF16), ((bm, bn), out_dtype)],
            temps=2 * bm * bn * 4),
        name=name,
    )(x, w)


def _matmul_kacc_kernel(h_ref, w_ref, x_ref, o_ref, *, alpha):
    @pl.when(pl.program_id(2) == 0)
    def _():
        o_ref[...] = alpha * x_ref[...]

    o_ref[...] += jnp.dot(h_ref[...], w_ref[...], preferred_element_type=F32)


def _matmul_kacc(h, w, layer, x, alpha, name):
    m, k = h.shape
    n = w.shape[2]
    bm = _tile(m, 1024)
    bn = _tile(n, 1024)
    bk = _tile(k, 2048)
    return pl.pallas_call(
        functools.partial(_matmul_kacc_kernel, alpha=alpha),
        grid=(m // bm, n // bn, k // bk),
        in_specs=[pl.BlockSpec((bm, bk), lambda i, j, kk: (i, kk)),
                  pl.BlockSpec((None, bk, bn), lambda i, j, kk: (layer, kk, j)),
                  pl.BlockSpec((bm, bn), lambda i, j, kk: (i, j))],
        out_specs=pl.BlockSpec((bm, bn), lambda i, j, kk: (i, j)),
        out_shape=jax.ShapeDtypeStruct((m, n), F32),
        compiler_params=_params(
            ("parallel", "parallel", "arbitrary"),
            [((bm, bk), BF16), ((bk, bn), BF16), ((bm, bn), F32), ((bm, bn), F32)],
            temps=bm * bn * 4),
        name=name,
    )(h, w, x)


def _layer_norm_rows(z, g, b):
    mu = jnp.mean(z, axis=1, keepdims=True)
    d = z - mu
    var = jnp.mean(d * d, axis=1, keepdims=True)
    return d * lax.rsqrt(var + LN_EPS) * g + b


def _ln_kernel(z_ref, g_ref, b_ref, o_ref, ob_ref):
    y = _layer_norm_rows(z_ref[...], g_ref[...], b_ref[...])
    o_ref[...] = y
    ob_ref[...] = y.astype(ob_ref.dtype)


def _layer_norm(z, g, b):
    m, d = z.shape
    bm = _tile(m, 256)
    row = pl.BlockSpec((bm, d), lambda i: (i, 0))
    vec = pl.BlockSpec((1, d), lambda i: (0, 0))
    return pl.pallas_call(
        _ln_kernel,
        grid=(m // bm,),
        in_specs=[row, vec, vec],
        out_specs=[row, row],
        out_shape=[jax.ShapeDtypeStruct((m, d), F32), jax.ShapeDtypeStruct((m, d), BF16)],
        compiler_params=_params(("parallel",), [((bm, d), F32)] * 2 + [((bm, d), BF16)],
                                temps=3 * bm * d * 4),
        name="layer_norm",
    )(z, g.reshape(1, d), b.reshape(1, d))


def _cross_kernel(z_ref, kv_ref, wq_ref, wo_ref, g1_ref, b1_ref, g_ref, b_ref, o_ref, ob_ref, *, alpha):
    x = _layer_norm_rows(z_ref[...], g1_ref[...], b1_ref[...])
    q = (jnp.dot(x.astype(BF16), wq_ref[...], preferred_element_type=F32) * ATTN_SCALE).astype(BF16)
    width = N_XHEADS * HEAD_DIM
    heads = []
    for h in range(N_XHEADS):
        lo, hi = h * HEAD_DIM, (h + 1) * HEAD_DIM
        s = lax.dot_general(q[:, lo:hi], kv_ref[:, lo:hi], (((1,), (1,)), ((), ())),
                            preferred_element_type=F32)
        p = jnp.exp(s - jnp.max(s, axis=1, keepdims=True))
        o = jnp.dot(p.astype(BF16), kv_ref[:, width + lo:width + hi], preferred_element_type=F32)
        heads.append((o / jnp.sum(p, axis=1, keepdims=True)).astype(BF16))
    f = jnp.dot(jnp.concatenate(heads, axis=1), wo_ref[...], preferred_element_type=F32)
    y = _layer_norm_rows(alpha * x + f, g_ref[...], b_ref[...])
    o_ref[...] = y
    ob_ref[...] = y.astype(ob_ref.dtype)


def _cross_attention(z, kv, wq, wo, layer, g1, b1, g, b, alpha, seq):
    m, d = z.shape
    n_mem = kv.shape[0] // (m // seq)
    width = N_XHEADS * HEAD_DIM
    bm = _tile(seq, 256)
    per_batch = seq // bm
    row_b = pl.BlockSpec((bm, d), lambda i: (i, 0))
    vec = pl.BlockSpec((1, d), lambda i: (0, 0))
    return pl.pallas_call(
        functools.partial(_cross_kernel, alpha=alpha),
        grid=(m // bm,),
        in_specs=[row_b,
                  pl.BlockSpec((n_mem, 2 * width), lambda i: (i // per_batch, 0)),
                  pl.BlockSpec((None, d, width), lambda i: (layer, 0, 0)),
                  pl.BlockSpec((None, width, d), lambda i: (layer, 0, 0)),
                  vec, vec, vec, vec],
        out_specs=[row_b, row_b],
        out_shape=[jax.ShapeDtypeStruct((m, d), F32), jax.ShapeDtypeStruct((m, d), BF16)],
        compiler_params=_params(
            ("parallel",),
            [((bm, d), BF16)] + [((bm, d), F32)] * 2
            + [((n_mem, 2 * width), BF16), ((d, width), BF16), ((width, d), BF16)],
            temps=6 * bm * d * 4),
        name="cross_attention",
    )(z, kv, wq, wo, g1.reshape(1, d), b1.reshape(1, d), g.reshape(1, d), b.reshape(1, d))


def kernel(x, mem, positions, w_in, b_forget, g_fox, g_dil, w_out, ln1_g, ln1_b,
           w_xq, w_xk, w_xv, w_xo, ln2_g, ln2_b, w_up, w_down, ln3_g, ln3_b):
    bsz, seq, d_model = x.shape
    depth = w_in.shape[0]
    nh = d_model // (2 * HEAD_DIM)
    width = nh * HEAD_DIM
    alpha = (2 * depth) ** 0.25
    tokens = bsz * seq
    assert w_in.shape[2] == 6 * width + nh and w_xq.shape[2] == N_XHEADS * HEAD_DIM
    assert seq % MAX_WINDOW == 0

    cos, sa, sb = _rope_tables(positions)
    xf = x.reshape(tokens, d_model)
    xb = xf.astype(BF16)
    memb = mem.reshape(-1, d_model).astype(BF16)

    w_main = jnp.concatenate([w_in[:, :, :3 * width], w_in[:, :, 3 * width + nh:]], axis=2).astype(BF16)
    w_ft = jnp.swapaxes(w_in[:, :, 3 * width:3 * width + nh], 1, 2).astype(BF16)
    w_kv = jnp.concatenate([w_xk, w_xv], axis=2).astype(BF16)
    w_out_b, w_xq_b, w_xo_b = w_out.astype(BF16), w_xq.astype(BF16), w_xo.astype(BF16)
    w_up_b, w_down_b = w_up.astype(BF16), w_down.astype(BF16)

    for l in range(depth):
        qkv = _in_projection(xb, w_main, l, cos, sa, sb, width)
        c = _forget_cumsum(xb, w_ft, l, b_forget[l], bsz, seq)
        y_fox = _fox_attention(qkv, c, g_fox[l], bsz, seq, nh)
        y_dil = _dilated_attention(qkv, g_dil[l], bsz, seq, nh, 3 * nh)
        z = _out_projection(y_fox, y_dil, w_out_b, l, xf, alpha)

        kv = _matmul(memb, w_kv, l, BF16, name="memory_kv")
        xf, xb = _cross_attention(z, kv, w_xq_b, w_xo_b, l, ln1_g[l], ln1_b[l],
                                  ln2_g[l], ln2_b[l], alpha, seq)

        hidden = _matmul(xb, w_up_b, l, BF16, activation="relu2", name="mlp_up")
        z = _matmul_kacc(hidden, w_down_b, l, xf, alpha, name="mlp_down")
        xf, xb = _layer_norm(z, ln3_g[l], ln3_b[l])

    return xf.reshape(bsz, seq, d_model)
```

```python
import functools
import math

import jax
import jax.numpy as jnp
import numpy as np
from jax import lax
from jax.experimental import pallas as pl
from jax.experimental.pallas import tpu as pltpu

F32 = jnp.float32
BF16 = jnp.bfloat16

HEAD_DIM = 128
ROPE_DIM = HEAD_DIM // 4
ROPE_HALF = ROPE_DIM // 2
ROPE_THETA = 500000.0
DILATION_CONFIGS = ((128, 1), (512, 4), (2048, 16))
MAX_WINDOW = max(w for w, _ in DILATION_CONFIGS)
N_XHEADS = 4
LN_EPS = 1e-5
NEG_INF = -1e30
ATTN_SCALE = HEAD_DIM ** -0.5
LOG2E = math.log2(math.e)
Q_SCALE_LOG2 = ATTN_SCALE * LOG2E

V7X_VMEM_BYTES = 64 * 2**20
LANES = 128


def _nbytes(shape, dtype):
    return math.prod(shape) * jnp.dtype(dtype).itemsize


def _params(semantics, blocks, temps=0):
    need = 2 * sum(_nbytes(s, d) for s, d in blocks) + temps
    limit = min(need + need // 4 + (2 << 20), V7X_VMEM_BYTES - (4 << 20))
    return pltpu.CompilerParams(dimension_semantics=semantics, vmem_limit_bytes=int(limit))


def _tile(dim, pref):
    t = min(dim, pref)
    while dim % t:
        t //= 2
    return t


SEG_Q_FOX, SEG_K_FOX, SEG_V_FOX, SEG_Q_DIL, SEG_K_DIL, SEG_V_DIL = range(6)


def _rope_table_kernel(pos_ref, inv_ref, cos_ref, sa_ref, sb_ref):
    ang = pos_ref[...].astype(F32) * inv_ref[...]
    lane = lax.broadcasted_iota(jnp.int32, ang.shape, 1)
    cos = jnp.cos(ang)
    sin = jnp.sin(ang)
    sa = jnp.where(lane < ROPE_HALF, -sin, 0.0)
    sb = jnp.where((lane >= ROPE_HALF) & (lane < ROPE_DIM), sin, 0.0)
    for ref, t in ((cos_ref, cos), (sa_ref, sa), (sb_ref, sb)):
        ref[0] = t * Q_SCALE_LOG2
        ref[1] = t


def _rope_tables(positions):
    t = positions.size
    rows = _tile(t, 2048)
    inv = ROPE_THETA ** (-jnp.arange(0, ROPE_DIM, 2, dtype=F32) / ROPE_DIM)
    inv_lane = jnp.concatenate([inv, inv, jnp.zeros((LANES - ROPE_DIM,), F32)]).reshape(1, LANES)
    out = jax.ShapeDtypeStruct((2, t, LANES), F32)
    blk = pl.BlockSpec((2, rows, LANES), lambda i: (0, i, 0))
    return pl.pallas_call(
        _rope_table_kernel,
        grid=(t // rows,),
        in_specs=[pl.BlockSpec((rows, 1), lambda i: (i, 0)),
                  pl.BlockSpec((1, LANES), lambda i: (0, 0))],
        out_specs=[blk, blk, blk],
        out_shape=[out, out, out],
        compiler_params=_params(("parallel",), [((rows, LANES), F32)] + [((2, rows, LANES), F32)] * 3,
                                temps=8 * rows * LANES * 4),
        name="rope_tables",
    )(positions.reshape(t, 1), inv_lane)


def _inproj_kernel(x_ref, w_ref, cos_ref, sa_ref, sb_ref, o_ref, *, tiles_per_seg):
    seg = pl.program_id(1) // tiles_per_seg
    rotary = (seg == SEG_Q_DIL) | (seg == SEG_K_DIL)

    @pl.when(rotary)
    def _():
        acc = jnp.dot(x_ref[...], w_ref[...], preferred_element_type=F32)
        cos, sa, sb = cos_ref[...], sa_ref[...], sb_ref[...]
        for c in range(acc.shape[1] // HEAD_DIM):
            cols = slice(c * HEAD_DIM, (c + 1) * HEAD_DIM)
            blk = acc[:, cols]
            out = (blk * cos + pltpu.roll(blk, HEAD_DIM - ROPE_HALF, 1) * sa
                   + pltpu.roll(blk, ROPE_HALF, 1) * sb)
            o_ref[:, cols] = out.astype(o_ref.dtype)

    @pl.when(jnp.logical_not(rotary))
    def _():
        acc = jnp.dot(x_ref[...], w_ref[...], preferred_element_type=F32)
        scale = jnp.where(seg == SEG_Q_FOX, Q_SCALE_LOG2, 1.0).astype(F32)
        o_ref[...] = (acc * scale).astype(o_ref.dtype)


def _in_projection(xb, w, layer, cos, sa, sb, seg_width):
    m, k = xb.shape
    n = w.shape[2]
    bm = _tile(m, 1024)
    bn = _tile(seg_width, 1024)
    tiles_per_seg = seg_width // bn

    def table_set(j):
        return jnp.where(j // tiles_per_seg == SEG_Q_DIL, 0, 1)

    tab = pl.BlockSpec((None, bm, LANES), lambda i, j: (table_set(j), i, 0))
    return pl.pallas_call(
        functools.partial(_inproj_kernel, tiles_per_seg=tiles_per_seg),
        grid=(m // bm, n // bn),
        in_specs=[pl.BlockSpec((bm, k), lambda i, j: (i, 0)),
                  pl.BlockSpec((None, k, bn), lambda i, j: (layer, 0, j)),
                  tab, tab, tab],
        out_specs=pl.BlockSpec((bm, bn), lambda i, j: (i, j)),
        out_shape=jax.ShapeDtypeStruct((m, n), BF16),
        compiler_params=_params(
            ("parallel", "arbitrary"),
            [((bm, k), BF16), ((k, bn), BF16), ((bm, bn), BF16)] + [((bm, LANES), F32)] * 3,
            temps=2 * bm * bn * 4),
        name="in_projection",
    )(xb, w, cos, sa, sb)


def _forget_kernel(x_ref, w_ref, b_ref, c_ref, carry_ref):
    @pl.when(pl.program_id(1) == 0)
    def _():
        carry_ref[...] = jnp.zeros_like(carry_ref)

    z = jnp.dot(x_ref[...], w_ref[...], preferred_element_type=F32) + b_ref[...]
    log_f = jnp.minimum(z, 0.0) - jnp.log1p(jnp.exp(-jnp.abs(z)))
    ts = z.shape[0]
    row = lax.broadcasted_iota(jnp.int32, (ts, ts), 0)
    col = lax.broadcasted_iota(jnp.int32, (ts, ts), 1)
    lower = (col <= row).astype(BF16)
    hi = log_f.astype(BF16)
    r1 = log_f - hi.astype(F32)
    mid = r1.astype(BF16)
    lo = (r1 - mid.astype(F32)).astype(BF16)
    cum = (jnp.dot(lower, hi, preferred_element_type=F32)
           + jnp.dot(lower, mid, preferred_element_type=F32)
           + jnp.dot(lower, lo, preferred_element_type=F32))
    c = cum + carry_ref[...]
    c_ref[...] = c * LOG2E
    carry_ref[...] = c[ts - 1:ts, :]


def _forget_cumsum(xb, w_f, layer, b_f, bsz, seq):
    _, k, nh = w_f.shape
    ts = _tile(seq, 512)
    ns = seq // ts
    c = pl.pallas_call(
        _forget_kernel,
        grid=(bsz, ns),
        in_specs=[pl.BlockSpec((ts, k), lambda b, s: (b * ns + s, 0)),
                  pl.BlockSpec((None, k, nh), lambda b, s: (layer, 0, 0)),
                  pl.BlockSpec((1, nh), lambda b, s: (0, 0))],
        out_specs=pl.BlockSpec((None, ts, nh), lambda b, s: (b, s, 0)),
        out_shape=jax.ShapeDtypeStruct((bsz, seq, nh), F32),
        scratch_shapes=[pltpu.VMEM((1, nh), F32)],
        compiler_params=_params(
            ("parallel", "arbitrary"),
            [((ts, k), BF16), ((k, LANES), BF16), ((ts, LANES), F32)],
            temps=4 * ts * ts * 4),
        name="forget_cumsum",
    )(xb, w_f, b_f.reshape(1, nh))
    return jnp.swapaxes(c, 1, 2).reshape(bsz, nh, 1, seq)


ATTN_CHUNK = 128


ATTN_HEADS = 4
ATTN_TILE = 512


def _head_cols(g):
    return slice(g * HEAD_DIM, (g + 1) * HEAD_DIM)


def _flash_tiles(q_ref, k_ref, v_ref, scratch, n_tiles, key_start, bias, mask_last):
    s_ref, mx_ref, m_ref, accl_ref = scratch
    tq = q_ref.shape[0]
    heads = q_ref.shape[1] // HEAD_DIM
    units = [(g, slice(r, r + ATTN_CHUNK)) for g in range(heads) for r in range(0, tq, ATTN_CHUNK)]
    m_ref[...] = jnp.full_like(m_ref, NEG_INF)
    accl_ref[...] = jnp.zeros_like(accl_ref)

    def keys(t):
        return k_ref[pl.ds(key_start(t), tq), :]

    def values(t):
        v = v_ref[pl.ds(key_start(t), tq), :]
        ones = jnp.ones((tq, HEAD_DIM), v.dtype)
        return [jnp.concatenate([v[:, _head_cols(g)], ones], axis=1) for g in range(heads)]

    def produce(t, k, g, rows, masked):
        s = lax.dot_general(q_ref[rows, _head_cols(g)], k[:, _head_cols(g)],
                            (((1,), (1,)), ((), ())), preferred_element_type=F32) + bias(t, g, rows, masked)
        s_ref[g, rows, :] = s
        mx_ref[g, rows, :] = jnp.broadcast_to(jnp.max(s, axis=1, keepdims=True), (ATTN_CHUNK, LANES))

    def consume(v_aug, g, rows):
        m_prev = m_ref[g, rows, :]
        m_next = jnp.maximum(m_prev, mx_ref[g, rows, :])
        p = jnp.exp2(s_ref[g, rows, :] - jnp.tile(m_next, (1, tq // LANES)))
        alpha = jnp.exp2(m_prev - m_next)
        pv = jnp.dot(p.astype(v_aug[g].dtype), v_aug[g], preferred_element_type=F32)
        accl_ref[g, rows, :] = jnp.tile(alpha, (1, 2)) * accl_ref[g, rows, :] + pv
        m_ref[g, rows, :] = m_next

    def produce_tile(t, masked):
        k = keys(t)
        for g, rows in units:
            produce(t, k, g, rows, masked)

    def advance(t, masked):
        v_aug = values(t)
        k = keys(t + 1)
        for g, rows in units:
            consume(v_aug, g, rows)
            produce(t + 1, k, g, rows, masked)

    def body(t, carry):
        advance(t, False)
        return carry

    if mask_last:
        pl.when(n_tiles > 1)(lambda: produce_tile(0, False))
        pl.when(n_tiles == 1)(lambda: produce_tile(0, True))
        lax.fori_loop(0, n_tiles - 2, body, 0)
        pl.when(n_tiles > 1)(lambda: advance(n_tiles - 2, True))
    else:
        produce_tile(0, False)
        lax.fori_loop(0, n_tiles - 1, body, 0)
    v_aug = values(n_tiles - 1)
    for g, rows in units:
        consume(v_aug, g, rows)


def _finish_heads(accl_ref, g_ref, o_ref):
    for g in range(accl_ref.shape[0]):
        o = accl_ref[g, :, :HEAD_DIM] / accl_ref[g, :, HEAD_DIM:]
        y = o * lax.rsqrt(jnp.mean(o * o, axis=1, keepdims=True) + LN_EPS)
        o_ref[:, _head_cols(g)] = (y * g_ref[:, _head_cols(g)]).astype(o_ref.dtype)


def _attention_call(kernel_fn, qkv, extra, extra_spec, g, bsz, seq, nh, col0, extra_bytes, name):
    tq = _tile(seq, ATTN_TILE)
    nq = seq // tq
    heads = math.gcd(ATTN_HEADS, nh)
    wide = heads * HEAD_DIM
    groups = nh // heads
    first = col0 // heads
    assert col0 % heads == 0 and tq % ATTN_CHUNK == 0
    return pl.pallas_call(
        kernel_fn,
        grid=(bsz, groups, nq),
        in_specs=[pl.BlockSpec((tq, wide), lambda b, h, i: (b * nq + i, first + h)),
                  pl.BlockSpec((seq, wide), lambda b, h, i: (b, first + groups + h)),
                  pl.BlockSpec((seq, wide), lambda b, h, i: (b, first + 2 * groups + h)),
                  extra_spec(heads),
                  pl.BlockSpec((1, wide), lambda b, h, i: (0, h))],
        out_specs=pl.BlockSpec((tq, wide), lambda b, h, i: (b * nq + i, h)),
        out_shape=jax.ShapeDtypeStruct((bsz * seq, nh * HEAD_DIM), BF16),
        scratch_shapes=[pltpu.VMEM((heads, tq, tq), F32),
                        pltpu.VMEM((heads, tq, LANES), F32),
                        pltpu.VMEM((heads, tq, LANES), F32),
                        pltpu.VMEM((heads, tq, 2 * HEAD_DIM), F32)],
        compiler_params=_params(
            ("parallel", "parallel", "arbitrary"),
            [((tq, wide), BF16)] * 2 + [((seq, wide), BF16)] * 2,
            temps=2 * extra_bytes + heads * 10 * tq * tq * 4),
        name=name,
    )(qkv, qkv, qkv, extra, g.reshape(1, nh * HEAD_DIM))


def _fox_kernel(q_ref, k_ref, v_ref, c_ref, g_ref, o_ref, *scratch):
    qi = pl.program_id(2)
    tq = q_ref.shape[0]

    def key_start(t):
        return pl.multiple_of(t * tq, tq)

    def bias(t, g, rows, masked):
        decay = -c_ref[g, :, pl.ds(key_start(t), tq)]
        if not masked:
            return decay
        row = rows.start + lax.broadcasted_iota(jnp.int32, (ATTN_CHUNK, tq), 0)
        col = lax.broadcasted_iota(jnp.int32, (ATTN_CHUNK, tq), 1)
        return jnp.where(col <= row, decay, NEG_INF)

    _flash_tiles(q_ref, k_ref, v_ref, scratch, qi + 1, key_start, bias, mask_last=True)
    _finish_heads(scratch[-1], g_ref, o_ref)


def _fox_attention(qkv, c, g, bsz, seq, nh):
    return _attention_call(
        _fox_kernel, qkv, c,
        lambda heads: pl.BlockSpec((None, heads, 1, seq), lambda b, h, i: (b, h, 0, 0)),
        g, bsz, seq, nh, 0, extra_bytes=ATTN_HEADS * 8 * seq * 4, name="fox_attention")


def _dilated_bias_tables(tq):
    n_off = MAX_WINDOW // tq + 1
    off = np.arange(n_off)[:, None, None] * tq
    dist = off + np.arange(tq)[None, :, None] - np.arange(tq)[None, None, :]
    count = np.zeros(dist.shape, np.float64)
    for window, dilation in DILATION_CONFIGS:
        count += (dist >= 0) & (dist <= window) & (dist % dilation == 0)
    return np.where(count > 0, np.log2(np.maximum(count, 1.0)), NEG_INF).astype(np.float32)


def _dilated_kernel(q_ref, k_ref, v_ref, bias_ref, g_ref, o_ref, *scratch):
    qi = pl.program_id(2)
    tq = q_ref.shape[0]
    n_tiles = jnp.minimum(qi + 1, bias_ref.shape[0])

    def key_start(t):
        return pl.multiple_of((qi - t) * tq, tq)

    def bias(t, g, rows, masked):
        return bias_ref[t, rows, :]

    _flash_tiles(q_ref, k_ref, v_ref, scratch, n_tiles, key_start, bias, mask_last=False)
    _finish_heads(scratch[-1], g_ref, o_ref)


def _dilated_attention(qkv, g, bsz, seq, nh, col0):
    tq = _tile(seq, ATTN_TILE)
    assert MAX_WINDOW % tq == 0
    bias = jnp.asarray(_dilated_bias_tables(tq))
    return _attention_call(
        _dilated_kernel, qkv, bias,
        lambda heads: pl.BlockSpec(bias.shape, lambda b, h, i: (0, 0, 0)),
        g, bsz, seq, nh, col0, extra_bytes=bias.size * 4, name="dilated_attention")


def _outproj_kernel(a_ref, b_ref, wa_ref, wb_ref, x_ref, o_ref, *, alpha):
    o_ref[...] = (alpha * x_ref[...]
                  + jnp.dot(a_ref[...], wa_ref[...], preferred_element_type=F32)
                  + jnp.dot(b_ref[...], wb_ref[...], preferred_element_type=F32))


def _out_projection(ya, yb, w, layer, x, alpha):
    m, ka = ya.shape
    kb = yb.shape[1]
    assert ka == kb and w.shape[1] == ka + kb
    n = w.shape[2]
    bm = _tile(m, 1024)
    bn = _tile(n, 1024)
    return pl.pallas_call(
        functools.partial(_outproj_kernel, alpha=alpha),
        grid=(m // bm, n // bn),
        in_specs=[pl.BlockSpec((bm, ka), lambda i, j: (i, 0)),
                  pl.BlockSpec((bm, kb), lambda i, j: (i, 0)),
                  pl.BlockSpec((None, ka, bn), lambda i, j: (layer, 0, j)),
                  pl.BlockSpec((None, kb, bn), lambda i, j: (layer, 1, j)),
                  pl.BlockSpec((bm, bn), lambda i, j: (i, j))],
        out_specs=pl.BlockSpec((bm, bn), lambda i, j: (i, j)),
        out_shape=jax.ShapeDtypeStruct((m, n), F32),
        compiler_params=_params(
            ("parallel", "arbitrary"),
            [((bm, ka), BF16)] * 2 + [((ka, bn), BF16)] * 2 + [((bm, bn), F32)] * 2,
            temps=bm * bn * 4),
        name="out_projection",
    )(ya, yb, w, w, x)


def _weight_spec(w, layer, block, index):
    if w.ndim == 2:
        return pl.BlockSpec(block, index)
    return pl.BlockSpec((None,) + block, lambda *g: (layer,) + tuple(index(*g)))


def _cast_job(stack, layer, steps, step_index):
    _, rows, cols = stack.shape
    slab = rows // steps
    assert rows % steps == 0 and slab % 16 == 0
    return (pl.BlockSpec((None, slab, cols), lambda *g: (layer, step_index(*g), 0)),
            pl.BlockSpec((slab, cols), lambda *g: (step_index(*g), 0)),
            jax.ShapeDtypeStruct((rows, cols), BF16),
            [((slab, cols), F32), ((slab, cols), BF16)])


def _matmul_kernel(x_ref, w_ref, *rest, activation):
    o_ref = rest[-2] if len(rest) == 3 else rest[0]
    acc = jnp.dot(x_ref[...], w_ref[...], preferred_element_type=F32)
    if activation == "relu2":
        r = jnp.maximum(acc, 0.0)
        acc = r * r
    o_ref[...] = acc.astype(o_ref.dtype)
    if len(rest) == 3:
        rest[2][...] = rest[0][...].astype(BF16)


def _matmul(x, w, layer, out_dtype, activation=None, cast=None, name="matmul"):
    m, k = x.shape
    n = w.shape[-1]
    bm = _tile(m, 1024)
    bn = _tile(n, 1024)
    nj = n // bn
    in_specs = [pl.BlockSpec((bm, k), lambda i, j: (i, 0)),
                _weight_spec(w, layer, (k, bn), lambda i, j: (0, j))]
    out_specs = [pl.BlockSpec((bm, bn), lambda i, j: (i, j))]
    out_shape = [jax.ShapeDtypeStruct((m, n), out_dtype)]
    blocks = [((bm, k), BF16), ((k, bn), BF16), ((bm, bn), out_dtype)]
    args = [x, w]
    if cast is not None:
        c_in, c_out, c_shape, c_blocks = _cast_job(cast[0], cast[1], (m // bm) * nj, lambda i, j: i * nj + j)
        in_specs.append(c_in)
        out_specs.append(c_out)
        out_shape.append(c_shape)
        blocks += c_blocks
        args.append(cast[0])
    out = pl.pallas_call(
        functools.partial(_matmul_kernel, activation=activation),
        grid=(m // bm, nj),
        in_specs=in_specs, out_specs=out_specs, out_shape=out_shape,
        compiler_params=_params(("parallel", "arbitrary"), blocks, temps=2 * bm * bn * 4),
        name=name,
    )(*args)
    return out[0] if cast is None else out


def _matmul_kacc_kernel(h_ref, w_ref, x_ref, *rest, alpha):
    o_ref = rest[-2] if len(rest) == 3 else rest[0]

    @pl.when(pl.program_id(2) == 0)
    def _():
        o_ref[...] = alpha * x_ref[...]

    o_ref[...] += jnp.dot(h_ref[...], w_ref[...], preferred_element_type=F32)
    if len(rest) == 3:
        rest[2][...] = rest[0][...].astype(BF16)


def _matmul_kacc(h, w, layer, x, alpha, cast=None, name="matmul_kacc"):
    m, k = h.shape
    n = w.shape[-1]
    bm = _tile(m, 1024)
    bn = _tile(n, 1024)
    bk = _tile(k, 4096)
    nj, nk = n // bn, k // bk
    in_specs = [pl.BlockSpec((bm, bk), lambda i, j, kk: (i, kk)),
                _weight_spec(w, layer, (bk, bn), lambda i, j, kk: (kk, j)),
                pl.BlockSpec((bm, bn), lambda i, j, kk: (i, j))]
    out_specs = [pl.BlockSpec((bm, bn), lambda i, j, kk: (i, j))]
    out_shape = [jax.ShapeDtypeStruct((m, n), F32)]
    blocks = [((bm, bk), BF16), ((bk, bn), BF16), ((bm, bn), F32), ((bm, bn), F32)]
    args = [h, w, x]
    if cast is not None:
        c_in, c_out, c_shape, c_blocks = _cast_job(cast[0], cast[1], (m // bm) * nj * nk,
                                                   lambda i, j, kk: (i * nj + j) * nk + kk)
        in_specs.append(c_in)
        out_specs.append(c_out)
        out_shape.append(c_shape)
        blocks += c_blocks
        args.append(cast[0])
    out = pl.pallas_call(
        functools.partial(_matmul_kacc_kernel, alpha=alpha),
        grid=(m // bm, nj, nk),
        in_specs=in_specs, out_specs=out_specs, out_shape=out_shape,
        compiler_params=_params(("parallel", "parallel", "arbitrary"), blocks, temps=bm * bn * 4),
        name=name,
    )(*args)
    return out[0] if cast is None else out


def _w_in_kernel(w_ref, o_ref, f_ref, *, split, skip):
    w = w_ref[...]
    o_ref[:, :split] = w[:, :split].astype(o_ref.dtype)
    o_ref[:, split:] = w[:, split + skip:].astype(o_ref.dtype)
    f_ref[...] = w[:, split:split + skip].astype(f_ref.dtype)


def _prepare_w_in(w_in, split, skip):
    depth, d, cols = w_in.shape
    out_cols = cols - skip
    rows = _tile(d, 128)
    return pl.pallas_call(
        functools.partial(_w_in_kernel, split=split, skip=skip),
        grid=(depth, d // rows),
        in_specs=[pl.BlockSpec((None, rows, cols), lambda l, r: (l, r, 0))],
        out_specs=[pl.BlockSpec((None, rows, out_cols), lambda l, r: (l, r, 0)),
                   pl.BlockSpec((None, rows, skip), lambda l, r: (l, r, 0))],
        out_shape=[jax.ShapeDtypeStruct((depth, d, out_cols), BF16),
                   jax.ShapeDtypeStruct((depth, d, skip), BF16)],
        compiler_params=_params(("parallel", "parallel"),
                                [((rows, cols), F32), ((rows, out_cols), BF16), ((rows, LANES), BF16)],
                                temps=2 * rows * cols * 4),
        name="prepare_w_in",
    )(w_in)


def _layer_norm_rows(z, g, b):
    mu = jnp.mean(z, axis=1, keepdims=True)
    d = z - mu
    var = jnp.mean(d * d, axis=1, keepdims=True)
    return d * lax.rsqrt(var + LN_EPS) * g + b


def _ln_kernel(z_ref, g_ref, b_ref, o_ref, ob_ref):
    y = _layer_norm_rows(z_ref[...], g_ref[...], b_ref[...])
    o_ref[...] = y
    ob_ref[...] = y.astype(ob_ref.dtype)


def _layer_norm(z, g, b):
    m, d = z.shape
    bm = _tile(m, 256)
    row = pl.BlockSpec((bm, d), lambda i: (i, 0))
    vec = pl.BlockSpec((1, d), lambda i: (0, 0))
    return pl.pallas_call(
        _ln_kernel,
        grid=(m // bm,),
        in_specs=[row, vec, vec],
        out_specs=[row, row],
        out_shape=[jax.ShapeDtypeStruct((m, d), F32), jax.ShapeDtypeStruct((m, d), BF16)],
        compiler_params=_params(("parallel",), [((bm, d), F32)] * 2 + [((bm, d), BF16)],
                                temps=3 * bm * d * 4),
        name="layer_norm",
    )(z, g.reshape(1, d), b.reshape(1, d))


def _cross_kernel(z_ref, kv_ref, wq_ref, wo_ref, g1_ref, b1_ref, g_ref, b_ref, o_ref, ob_ref, *, alpha):
    x = _layer_norm_rows(z_ref[...], g1_ref[...], b1_ref[...])
    q = (jnp.dot(x.astype(BF16), wq_ref[...], preferred_element_type=F32) * ATTN_SCALE).astype(BF16)
    width = N_XHEADS * HEAD_DIM
    heads = []
    for h in range(N_XHEADS):
        lo, hi = h * HEAD_DIM, (h + 1) * HEAD_DIM
        s = lax.dot_general(q[:, lo:hi], kv_ref[:, lo:hi], (((1,), (1,)), ((), ())),
                            preferred_element_type=F32)
        p = jnp.exp(s - jnp.max(s, axis=1, keepdims=True))
        o = jnp.dot(p.astype(BF16), kv_ref[:, width + lo:width + hi], preferred_element_type=F32)
        heads.append((o / jnp.sum(p, axis=1, keepdims=True)).astype(BF16))
    f = jnp.dot(jnp.concatenate(heads, axis=1), wo_ref[...], preferred_element_type=F32)
    y = _layer_norm_rows(alpha * x + f, g_ref[...], b_ref[...])
    o_ref[...] = y
    ob_ref[...] = y.astype(ob_ref.dtype)


def _cross_attention(z, kv, wq, wo, layer, g1, b1, g, b, alpha, seq):
    m, d = z.shape
    n_mem = kv.shape[0] // (m // seq)
    width = N_XHEADS * HEAD_DIM
    bm = _tile(seq, 256)
    per_batch = seq // bm
    row_b = pl.BlockSpec((bm, d), lambda i: (i, 0))
    vec = pl.BlockSpec((1, d), lambda i: (0, 0))
    return pl.pallas_call(
        functools.partial(_cross_kernel, alpha=alpha),
        grid=(m // bm,),
        in_specs=[row_b,
                  pl.BlockSpec((n_mem, 2 * width), lambda i: (i // per_batch, 0)),
                  pl.BlockSpec((None, d, width), lambda i: (layer, 0, 0)),
                  pl.BlockSpec((None, width, d), lambda i: (layer, 0, 0)),
                  vec, vec, vec, vec],
        out_specs=[row_b, row_b],
        out_shape=[jax.ShapeDtypeStruct((m, d), F32), jax.ShapeDtypeStruct((m, d), BF16)],
        compiler_params=_params(
            ("parallel",),
            [((bm, d), BF16)] + [((bm, d), F32)] * 2
            + [((n_mem, 2 * width), BF16), ((d, width), BF16), ((width, d), BF16)],
            temps=6 * bm * d * 4),
        name="cross_attention",
    )(z, kv, wq, wo, g1.reshape(1, d), b1.reshape(1, d), g.reshape(1, d), b.reshape(1, d))


def kernel(x, mem, positions, w_in, b_forget, g_fox, g_dil, w_out, ln1_g, ln1_b,
           w_xq, w_xk, w_xv, w_xo, ln2_g, ln2_b, w_up, w_down, ln3_g, ln3_b):
    bsz, seq, d_model = x.shape
    depth = w_in.shape[0]
    nh = d_model // (2 * HEAD_DIM)
    width = nh * HEAD_DIM
    alpha = (2 * depth) ** 0.25
    tokens = bsz * seq
    assert w_in.shape[2] == 6 * width + nh and w_xq.shape[2] == N_XHEADS * HEAD_DIM
    assert seq % MAX_WINDOW == 0

    cos, sa, sb = _rope_tables(positions)
    xf = x.reshape(tokens, d_model)
    xb = xf.astype(BF16)
    memb = mem.reshape(-1, d_model).astype(BF16)

    w_main, w_f = _prepare_w_in(w_in, 3 * width, nh)
    w_kv = jnp.concatenate([w_xk, w_xv], axis=2).astype(BF16)
    w_out_b, w_xq_b, w_xo_b = w_out.astype(BF16), w_xq.astype(BF16), w_xo.astype(BF16)
    w_up_l = w_up[0].astype(BF16)

    for l in range(depth):
        qkv = _in_projection(xb, w_main, l, cos, sa, sb, width)
        c = _forget_cumsum(xb, w_f, l, b_forget[l], bsz, seq)
        y_fox = _fox_attention(qkv, c, g_fox[l], bsz, seq, nh)
        y_dil = _dilated_attention(qkv, g_dil[l], bsz, seq, nh, 3 * nh)
        z = _out_projection(y_fox, y_dil, w_out_b, l, xf, alpha)

        kv = _matmul(memb, w_kv, l, BF16, name="memory_kv")
        xf, xb = _cross_attention(z, kv, w_xq_b, w_xo_b, l, ln1_g[l], ln1_b[l],
                                  ln2_g[l], ln2_b[l], alpha, seq)

        hidden, w_down_l = _matmul(xb, w_up_l, None, BF16, activation="relu2", cast=(w_down, l), name="mlp_up")
        if l + 1 < depth:
            z, w_up_l = _matmul_kacc(hidden, w_down_l, None, xf, alpha, cast=(w_up, l + 1), name="mlp_down")
        else:
            z = _matmul_kacc(hidden, w_down_l, None, xf, alpha, name="mlp_down")
        xf, xb = _layer_norm(z, ln3_g[l], ln3_b[l])

    return xf.reshape(bsz, seq, d_model)
```

```python
import functools
import math

import jax
import jax.numpy as jnp
import numpy as np
from jax import lax
from jax.experimental import pallas as pl
from jax.experimental.pallas import tpu as pltpu

F32 = jnp.float32
BF16 = jnp.bfloat16

HEAD_DIM = 128
ROPE_DIM = HEAD_DIM // 4
ROPE_HALF = ROPE_DIM // 2
ROPE_THETA = 500000.0
DILATION_CONFIGS = ((128, 1), (512, 4), (2048, 16))
MAX_WINDOW = max(w for w, _ in DILATION_CONFIGS)
N_XHEADS = 4
LN_EPS = 1e-5
NEG_INF = -1e30
ATTN_SCALE = HEAD_DIM ** -0.5
LOG2E = math.log2(math.e)
Q_SCALE_LOG2 = ATTN_SCALE * LOG2E

V7X_VMEM_BYTES = 64 * 2**20
LANES = 128


def _nbytes(shape, dtype):
    return math.prod(shape) * jnp.dtype(dtype).itemsize


def _params(semantics, blocks, temps=0):
    need = 2 * sum(_nbytes(s, d) for s, d in blocks) + temps
    limit = min(need + need // 4 + (2 << 20), V7X_VMEM_BYTES - (4 << 20))
    return pltpu.CompilerParams(dimension_semantics=semantics, vmem_limit_bytes=int(limit))


def _tile(dim, pref):
    t = min(dim, pref)
    while dim % t:
        t //= 2
    return t


SEG_Q_FOX, SEG_K_FOX, SEG_V_FOX, SEG_Q_DIL, SEG_K_DIL, SEG_V_DIL = range(6)


def _rope_table_kernel(pos_ref, inv_ref, cos_ref, sa_ref, sb_ref):
    ang = pos_ref[...].astype(F32) * inv_ref[...]
    lane = lax.broadcasted_iota(jnp.int32, ang.shape, 1)
    cos = jnp.cos(ang)
    sin = jnp.sin(ang)
    sa = jnp.where(lane < ROPE_HALF, -sin, 0.0)
    sb = jnp.where((lane >= ROPE_HALF) & (lane < ROPE_DIM), sin, 0.0)
    for ref, t in ((cos_ref, cos), (sa_ref, sa), (sb_ref, sb)):
        ref[0] = t * Q_SCALE_LOG2
        ref[1] = t


def _rope_tables(positions):
    t = positions.size
    rows = _tile(t, 2048)
    inv = ROPE_THETA ** (-jnp.arange(0, ROPE_DIM, 2, dtype=F32) / ROPE_DIM)
    inv_lane = jnp.concatenate([inv, inv, jnp.zeros((LANES - ROPE_DIM,), F32)]).reshape(1, LANES)
    out = jax.ShapeDtypeStruct((2, t, LANES), F32)
    blk = pl.BlockSpec((2, rows, LANES), lambda i: (0, i, 0))
    return pl.pallas_call(
        _rope_table_kernel,
        grid=(t // rows,),
        in_specs=[pl.BlockSpec((rows, 1), lambda i: (i, 0)),
                  pl.BlockSpec((1, LANES), lambda i: (0, 0))],
        out_specs=[blk, blk, blk],
        out_shape=[out, out, out],
        compiler_params=_params(("parallel",), [((rows, LANES), F32)] + [((2, rows, LANES), F32)] * 3,
                                temps=8 * rows * LANES * 4),
        name="rope_tables",
    )(positions.reshape(t, 1), inv_lane)


def _inproj_kernel(x_ref, w_ref, cos_ref, sa_ref, sb_ref, o_ref, *, tiles_per_seg):
    seg = pl.program_id(1) // tiles_per_seg
    rotary = (seg == SEG_Q_DIL) | (seg == SEG_K_DIL)

    def project():
        return lax.dot_general(x_ref[...], w_ref[...], (((1,), (1,)), ((), ())), preferred_element_type=F32)

    @pl.when(rotary)
    def _():
        acc = project()
        cos, sa, sb = cos_ref[...], sa_ref[...], sb_ref[...]
        for c in range(acc.shape[1] // HEAD_DIM):
            cols = slice(c * HEAD_DIM, (c + 1) * HEAD_DIM)
            blk = acc[:, cols]
            out = (blk * cos + pltpu.roll(blk, HEAD_DIM - ROPE_HALF, 1) * sa
                   + pltpu.roll(blk, ROPE_HALF, 1) * sb)
            o_ref[:, cols] = out.astype(o_ref.dtype)

    @pl.when(jnp.logical_not(rotary))
    def _():
        scale = jnp.where(seg == SEG_Q_FOX, Q_SCALE_LOG2, 1.0).astype(F32)
        o_ref[...] = (project() * scale).astype(o_ref.dtype)


def _in_projection(xb, w_t, layer, cos, sa, sb, seg_width):
    m, k = xb.shape
    n = w_t.shape[1]
    bm = _tile(m, 1024)
    bn = _tile(seg_width, 1024)
    tiles_per_seg = seg_width // bn

    def table_set(j):
        return jnp.where(j // tiles_per_seg == SEG_Q_DIL, 0, 1)

    tab = pl.BlockSpec((None, bm, LANES), lambda i, j: (table_set(j), i, 0))
    return pl.pallas_call(
        functools.partial(_inproj_kernel, tiles_per_seg=tiles_per_seg),
        grid=(m // bm, n // bn),
        in_specs=[pl.BlockSpec((bm, k), lambda i, j: (i, 0)),
                  pl.BlockSpec((None, bn, k), lambda i, j: (layer, j, 0)),
                  tab, tab, tab],
        out_specs=pl.BlockSpec((bm, bn), lambda i, j: (i, j)),
        out_shape=jax.ShapeDtypeStruct((m, n), BF16),
        compiler_params=_params(
            ("parallel", "arbitrary"),
            [((bm, k), BF16), ((bn, k), BF16), ((bm, bn), BF16)] + [((bm, LANES), F32)] * 3,
            temps=2 * bm * bn * 4),
        name="in_projection",
    )(xb, w_t, cos, sa, sb)


def _forget_kernel(x_ref, w_ref, b_ref, c_ref, carry_ref):
    @pl.when(pl.program_id(1) == 0)
    def _():
        carry_ref[...] = jnp.zeros_like(carry_ref)

    z = lax.dot_general(w_ref[...], x_ref[...], (((1,), (1,)), ((), ())),
                        preferred_element_type=F32) + b_ref[...]
    log_f = jnp.minimum(z, 0.0) - jnp.log1p(jnp.exp(-jnp.abs(z)))
    ts = z.shape[1]
    row = lax.broadcasted_iota(jnp.int32, (ts, ts), 0)
    col = lax.broadcasted_iota(jnp.int32, (ts, ts), 1)
    upper = (row <= col).astype(BF16)
    hi = log_f.astype(BF16)
    r1 = log_f - hi.astype(F32)
    mid = r1.astype(BF16)
    lo = (r1 - mid.astype(F32)).astype(BF16)
    cum = (jnp.dot(hi, upper, preferred_element_type=F32)
           + jnp.dot(mid, upper, preferred_element_type=F32)
           + jnp.dot(lo, upper, preferred_element_type=F32))
    c = cum + carry_ref[:, 0:1]
    c_ref[...] = c * LOG2E
    carry_ref[...] = jnp.broadcast_to(c[:, ts - 1:ts], carry_ref.shape)


def _forget_cumsum(xb, w_ft, layer, b_f, bsz, seq):
    _, nh, k = w_ft.shape
    ts = _tile(seq, 512)
    ns = seq // ts
    c = pl.pallas_call(
        _forget_kernel,
        grid=(bsz, ns),
        in_specs=[pl.BlockSpec((ts, k), lambda b, s: (b * ns + s, 0)),
                  pl.BlockSpec((None, nh, k), lambda b, s: (layer, 0, 0)),
                  pl.BlockSpec((nh, 1), lambda b, s: (0, 0))],
        out_specs=pl.BlockSpec((None, nh, ts), lambda b, s: (b, 0, s)),
        out_shape=jax.ShapeDtypeStruct((bsz, nh, seq), F32),
        scratch_shapes=[pltpu.VMEM((nh, LANES), F32)],
        compiler_params=_params(
            ("parallel", "arbitrary"),
            [((ts, k), BF16), ((nh, k), BF16), ((nh, ts), F32)],
            temps=4 * ts * ts * 4),
        name="forget_cumsum",
    )(xb, w_ft, b_f.reshape(nh, 1))
    return c.reshape(bsz, nh, 1, seq)


ATTN_CHUNK = 128


ATTN_HEADS = 4
ATTN_TILE = 512


def _head_cols(g):
    return slice(g * HEAD_DIM, (g + 1) * HEAD_DIM)


def _flash_tiles(q_ref, k_ref, v_ref, scratch, n_tiles, key_start, bias, mask_last):
    s_ref, mx_ref, m_ref, accl_ref = scratch
    tq = q_ref.shape[0]
    heads = q_ref.shape[1] // HEAD_DIM
    units = [(g, slice(r, r + ATTN_CHUNK)) for g in range(heads) for r in range(0, tq, ATTN_CHUNK)]
    m_ref[...] = jnp.full_like(m_ref, NEG_INF)
    accl_ref[...] = jnp.zeros_like(accl_ref)

    def keys(t):
        return k_ref[pl.ds(key_start(t), tq), :]

    def values(t):
        v = v_ref[pl.ds(key_start(t), tq), :]
        ones = jnp.ones((tq, HEAD_DIM), v.dtype)
        return [jnp.concatenate([v[:, _head_cols(g)], ones], axis=1) for g in range(heads)]

    def produce(t, k, g, rows, masked):
        s = lax.dot_general(q_ref[rows, _head_cols(g)], k[:, _head_cols(g)],
                            (((1,), (1,)), ((), ())), preferred_element_type=F32) + bias(t, g, rows, masked)
        s_ref[g, rows, :] = s
        mx_ref[g, rows, :] = jnp.broadcast_to(jnp.max(s, axis=1, keepdims=True), (ATTN_CHUNK, LANES))

    def consume(v_aug, g, rows):
        m_prev = m_ref[g, rows, :]
        m_next = jnp.maximum(m_prev, mx_ref[g, rows, :])
        p = jnp.exp2(s_ref[g, rows, :] - jnp.tile(m_next, (1, tq // LANES)))
        alpha = jnp.exp2(m_prev - m_next)
        pv = jnp.dot(p.astype(v_aug[g].dtype), v_aug[g], preferred_element_type=F32)
        accl_ref[g, rows, :] = jnp.tile(alpha, (1, 2)) * accl_ref[g, rows, :] + pv
        m_ref[g, rows, :] = m_next

    def produce_tile(t, masked):
        k = keys(t)
        for g, rows in units:
            produce(t, k, g, rows, masked)

    def advance(t, masked):
        v_aug = values(t)
        k = keys(t + 1)
        for g, rows in units:
            consume(v_aug, g, rows)
            produce(t + 1, k, g, rows, masked)

    def body(t, carry):
        advance(t, False)
        return carry

    if mask_last:
        pl.when(n_tiles > 1)(lambda: produce_tile(0, False))
        pl.when(n_tiles == 1)(lambda: produce_tile(0, True))
        lax.fori_loop(0, n_tiles - 2, body, 0)
        pl.when(n_tiles > 1)(lambda: advance(n_tiles - 2, True))
    else:
        produce_tile(0, False)
        lax.fori_loop(0, n_tiles - 1, body, 0)
    v_aug = values(n_tiles - 1)
    for g, rows in units:
        consume(v_aug, g, rows)


def _finish_heads(accl_ref, g_ref, o_ref):
    for g in range(accl_ref.shape[0]):
        o = accl_ref[g, :, :HEAD_DIM] / accl_ref[g, :, HEAD_DIM:]
        y = o * lax.rsqrt(jnp.mean(o * o, axis=1, keepdims=True) + LN_EPS)
        o_ref[:, _head_cols(g)] = (y * g_ref[:, _head_cols(g)]).astype(o_ref.dtype)


def _attention_call(kernel_fn, qkv, extra, extra_spec, g, bsz, seq, nh, col0, extra_bytes, name):
    tq = _tile(seq, ATTN_TILE)
    nq = seq // tq
    heads = math.gcd(ATTN_HEADS, nh)
    wide = heads * HEAD_DIM
    groups = nh // heads
    first = col0 // heads
    assert col0 % heads == 0 and tq % ATTN_CHUNK == 0
    return pl.pallas_call(
        kernel_fn,
        grid=(bsz, groups, nq),
        in_specs=[pl.BlockSpec((tq, wide), lambda b, h, i: (b * nq + i, first + h)),
                  pl.BlockSpec((seq, wide), lambda b, h, i: (b, first + groups + h)),
                  pl.BlockSpec((seq, wide), lambda b, h, i: (b, first + 2 * groups + h)),
                  extra_spec(heads),
                  pl.BlockSpec((1, wide), lambda b, h, i: (0, h))],
        out_specs=pl.BlockSpec((tq, wide), lambda b, h, i: (b * nq + i, h)),
        out_shape=jax.ShapeDtypeStruct((bsz * seq, nh * HEAD_DIM), BF16),
        scratch_shapes=[pltpu.VMEM((heads, tq, tq), F32),
                        pltpu.VMEM((heads, tq, LANES), F32),
                        pltpu.VMEM((heads, tq, LANES), F32),
                        pltpu.VMEM((heads, tq, 2 * HEAD_DIM), F32)],
        compiler_params=_params(
            ("parallel", "parallel", "arbitrary"),
            [((tq, wide), BF16)] * 2 + [((seq, wide), BF16)] * 2,
            temps=2 * extra_bytes + heads * 10 * tq * tq * 4),
        name=name,
    )(qkv, qkv, qkv, extra, g.reshape(1, nh * HEAD_DIM))


def _fox_kernel(q_ref, k_ref, v_ref, c_ref, g_ref, o_ref, *scratch):
    qi = pl.program_id(2)
    tq = q_ref.shape[0]

    def key_start(t):
        return pl.multiple_of(t * tq, tq)

    def bias(t, g, rows, masked):
        decay = -c_ref[g, :, pl.ds(key_start(t), tq)]
        if not masked:
            return decay
        row = rows.start + lax.broadcasted_iota(jnp.int32, (ATTN_CHUNK, tq), 0)
        col = lax.broadcasted_iota(jnp.int32, (ATTN_CHUNK, tq), 1)
        return jnp.where(col <= row, decay, NEG_INF)

    _flash_tiles(q_ref, k_ref, v_ref, scratch, qi + 1, key_start, bias, mask_last=True)
    _finish_heads(scratch[-1], g_ref, o_ref)


def _fox_attention(qkv, c, g, bsz, seq, nh):
    return _attention_call(
        _fox_kernel, qkv, c,
        lambda heads: pl.BlockSpec((None, heads, 1, seq), lambda b, h, i: (b, h, 0, 0)),
        g, bsz, seq, nh, 0, extra_bytes=ATTN_HEADS * 8 * seq * 4, name="fox_attention")


def _dilated_bias_tables(tq):
    n_off = MAX_WINDOW // tq + 1
    off = np.arange(n_off)[:, None, None] * tq
    dist = off + np.arange(tq)[None, :, None] - np.arange(tq)[None, None, :]
    count = np.zeros(dist.shape, np.float64)
    for window, dilation in DILATION_CONFIGS:
        count += (dist >= 0) & (dist <= window) & (dist % dilation == 0)
    return np.where(count > 0, np.log2(np.maximum(count, 1.0)), NEG_INF).astype(np.float32)


def _dilated_kernel(q_ref, k_ref, v_ref, bias_ref, g_ref, o_ref, *scratch):
    qi = pl.program_id(2)
    tq = q_ref.shape[0]
    n_tiles = jnp.minimum(qi + 1, bias_ref.shape[0])

    def key_start(t):
        return pl.multiple_of((qi - t) * tq, tq)

    def bias(t, g, rows, masked):
        return bias_ref[t, rows, :]

    _flash_tiles(q_ref, k_ref, v_ref, scratch, n_tiles, key_start, bias, mask_last=False)
    _finish_heads(scratch[-1], g_ref, o_ref)


def _dilated_attention(qkv, g, bsz, seq, nh, col0):
    tq = _tile(seq, ATTN_TILE)
    assert MAX_WINDOW % tq == 0
    bias = jnp.asarray(_dilated_bias_tables(tq))
    return _attention_call(
        _dilated_kernel, qkv, bias,
        lambda heads: pl.BlockSpec(bias.shape, lambda b, h, i: (0, 0, 0)),
        g, bsz, seq, nh, col0, extra_bytes=bias.size * 4, name="dilated_attention")


def _outproj_kernel(a_ref, b_ref, wa_ref, wb_ref, x_ref, o_ref, *, alpha):
    o_ref[...] = (alpha * x_ref[...]
                  + jnp.dot(a_ref[...], wa_ref[...], preferred_element_type=F32)
                  + jnp.dot(b_ref[...], wb_ref[...], preferred_element_type=F32))


def _out_projection(ya, yb, w, layer, x, alpha):
    m, ka = ya.shape
    kb = yb.shape[1]
    assert ka == kb and w.shape[1] == ka + kb
    n = w.shape[2]
    bm = _tile(m, 1024)
    bn = _tile(n, 1024)
    return pl.pallas_call(
        functools.partial(_outproj_kernel, alpha=alpha),
        grid=(m // bm, n // bn),
        in_specs=[pl.BlockSpec((bm, ka), lambda i, j: (i, 0)),
                  pl.BlockSpec((bm, kb), lambda i, j: (i, 0)),
                  pl.BlockSpec((None, ka, bn), lambda i, j: (layer, 0, j)),
                  pl.BlockSpec((None, kb, bn), lambda i, j: (layer, 1, j)),
                  pl.BlockSpec((bm, bn), lambda i, j: (i, j))],
        out_specs=pl.BlockSpec((bm, bn), lambda i, j: (i, j)),
        out_shape=jax.ShapeDtypeStruct((m, n), F32),
        compiler_params=_params(
            ("parallel", "arbitrary"),
            [((bm, ka), BF16)] * 2 + [((ka, bn), BF16)] * 2 + [((bm, bn), F32)] * 2,
            temps=bm * bn * 4),
        name="out_projection",
    )(ya, yb, w, w, x)


def _weight_spec(w, layer, block, index):
    if w.ndim == 2:
        return pl.BlockSpec(block, index)
    return pl.BlockSpec((None,) + block, lambda *g: (layer,) + tuple(index(*g)))


def _cast_job(stack, layer, steps, step_index):
    _, rows, cols = stack.shape
    slab = rows // steps
    assert rows % steps == 0 and slab % 16 == 0
    return (pl.BlockSpec((None, slab, cols), lambda *g: (layer, step_index(*g), 0)),
            pl.BlockSpec((slab, cols), lambda *g: (step_index(*g), 0)),
            jax.ShapeDtypeStruct((rows, cols), BF16),
            [((slab, cols), F32), ((slab, cols), BF16)])


def _matmul_kernel(x_ref, w_ref, *rest, activation):
    o_ref = rest[-2] if len(rest) == 3 else rest[0]
    acc = jnp.dot(x_ref[...], w_ref[...], preferred_element_type=F32)
    if activation == "relu2":
        r = jnp.maximum(acc, 0.0)
        acc = r * r
    o_ref[...] = acc.astype(o_ref.dtype)
    if len(rest) == 3:
        rest[2][...] = rest[0][...].astype(BF16)


def _matmul(x, w, layer, out_dtype, activation=None, cast=None, name="matmul"):
    m, k = x.shape
    n = w.shape[-1]
    bm = _tile(m, 1024)
    bn = _tile(n, 1024)
    nj = n // bn
    in_specs = [pl.BlockSpec((bm, k), lambda i, j: (i, 0)),
                _weight_spec(w, layer, (k, bn), lambda i, j: (0, j))]
    out_specs = [pl.BlockSpec((bm, bn), lambda i, j: (i, j))]
    out_shape = [jax.ShapeDtypeStruct((m, n), out_dtype)]
    blocks = [((bm, k), BF16), ((k, bn), BF16), ((bm, bn), out_dtype)]
    args = [x, w]
    if cast is not None:
        c_in, c_out, c_shape, c_blocks = _cast_job(cast[0], cast[1], (m // bm) * nj, lambda i, j: i * nj + j)
        in_specs.append(c_in)
        out_specs.append(c_out)
        out_shape.append(c_shape)
        blocks += c_blocks
        args.append(cast[0])
    out = pl.pallas_call(
        functools.partial(_matmul_kernel, activation=activation),
        grid=(m // bm, nj),
        in_specs=in_specs, out_specs=out_specs, out_shape=out_shape,
        compiler_params=_params(("parallel", "arbitrary"), blocks, temps=2 * bm * bn * 4),
        name=name,
    )(*args)
    return out[0] if cast is None else out


def _matmul_kacc_kernel(h_ref, w_ref, x_ref, *rest, alpha):
    o_ref = rest[-2] if len(rest) == 3 else rest[0]

    @pl.when(pl.program_id(2) == 0)
    def _():
        o_ref[...] = alpha * x_ref[...]

    o_ref[...] += jnp.dot(h_ref[...], w_ref[...], preferred_element_type=F32)
    if len(rest) == 3:
        rest[2][...] = rest[0][...].astype(BF16)


def _matmul_kacc(h, w, layer, x, alpha, cast=None, name="matmul_kacc"):
    m, k = h.shape
    n = w.shape[-1]
    bm = _tile(m, 1024)
    bn = _tile(n, 1024)
    bk = _tile(k, 4096)
    nj, nk = n // bn, k // bk
    in_specs = [pl.BlockSpec((bm, bk), lambda i, j, kk: (i, kk)),
                _weight_spec(w, layer, (bk, bn), lambda i, j, kk: (kk, j)),
                pl.BlockSpec((bm, bn), lambda i, j, kk: (i, j))]
    out_specs = [pl.BlockSpec((bm, bn), lambda i, j, kk: (i, j))]
    out_shape = [jax.ShapeDtypeStruct((m, n), F32)]
    blocks = [((bm, bk), BF16), ((bk, bn), BF16), ((bm, bn), F32), ((bm, bn), F32)]
    args = [h, w, x]
    if cast is not None:
        c_in, c_out, c_shape, c_blocks = _cast_job(cast[0], cast[1], (m // bm) * nj * nk,
                                                   lambda i, j, kk: (i * nj + j) * nk + kk)
        in_specs.append(c_in)
        out_specs.append(c_out)
        out_shape.append(c_shape)
        blocks += c_blocks
        args.append(cast[0])
    out = pl.pallas_call(
        functools.partial(_matmul_kacc_kernel, alpha=alpha),
        grid=(m // bm, nj, nk),
        in_specs=in_specs, out_specs=out_specs, out_shape=out_shape,
        compiler_params=_params(("parallel", "parallel", "arbitrary"), blocks, temps=bm * bn * 4),
        name=name,
    )(*args)
    return out[0] if cast is None else out


def _w_in_kernel(a_ref, b_ref, f_src_ref, o_ref, f_ref, *, first_shifted, skip):
    rb = a_ref.shape[0]

    @pl.when(pl.program_id(1) < first_shifted)
    def _():
        o_ref[...] = a_ref[...].astype(o_ref.dtype)

    @pl.when(pl.program_id(1) >= first_shifted)
    def _():
        o_ref[:rb - skip, :] = a_ref[skip:, :].astype(o_ref.dtype)
        o_ref[rb - skip:, :] = b_ref[...].astype(o_ref.dtype)

    f_ref[...] = f_src_ref[...].astype(f_ref.dtype)


def _prepare_w_in(w_t, split, skip):
    depth, cols, d = w_t.shape
    out_rows = cols - skip
    rb = _tile(split, 512)
    assert split % rb == 0 and out_rows % rb == 0 and rb % skip == 0 and skip % 8 == 0
    per = rb // skip
    return pl.pallas_call(
        functools.partial(_w_in_kernel, first_shifted=split // rb, skip=skip),
        grid=(depth, out_rows // rb),
        in_specs=[pl.BlockSpec((None, rb, d), lambda l, r: (l, r, 0)),
                  pl.BlockSpec((None, skip, d), lambda l, r: (l, (r + 1) * per, 0)),
                  pl.BlockSpec((None, skip, d), lambda l, r: (l, split // skip, 0))],
        out_specs=[pl.BlockSpec((None, rb, d), lambda l, r: (l, r, 0)),
                   pl.BlockSpec((None, skip, d), lambda l, r: (l, 0, 0))],
        out_shape=[jax.ShapeDtypeStruct((depth, out_rows, d), BF16),
                   jax.ShapeDtypeStruct((depth, skip, d), BF16)],
        compiler_params=_params(("parallel", "arbitrary"),
                                [((rb, d), F32), ((rb, d), BF16)] + [((skip, d), F32)] * 3,
                                temps=rb * d * 4),
        name="prepare_w_in",
    )(w_t, w_t, w_t)


def _layer_norm_rows(z, g, b):
    mu = jnp.mean(z, axis=1, keepdims=True)
    d = z - mu
    var = jnp.mean(d * d, axis=1, keepdims=True)
    return d * lax.rsqrt(var + LN_EPS) * g + b


def _ln_kernel(z_ref, g_ref, b_ref, o_ref, ob_ref):
    y = _layer_norm_rows(z_ref[...], g_ref[...], b_ref[...])
    o_ref[...] = y
    ob_ref[...] = y.astype(ob_ref.dtype)


def _layer_norm(z, g, b):
    m, d = z.shape
    bm = _tile(m, 256)
    row = pl.BlockSpec((bm, d), lambda i: (i, 0))
    vec = pl.BlockSpec((1, d), lambda i: (0, 0))
    return pl.pallas_call(
        _ln_kernel,
        grid=(m // bm,),
        in_specs=[row, vec, vec],
        out_specs=[row, row],
        out_shape=[jax.ShapeDtypeStruct((m, d), F32), jax.ShapeDtypeStruct((m, d), BF16)],
        compiler_params=_params(("parallel",), [((bm, d), F32)] * 2 + [((bm, d), BF16)],
                                temps=3 * bm * d * 4),
        name="layer_norm",
    )(z, g.reshape(1, d), b.reshape(1, d))


def _cross_kernel(z_ref, kv_ref, wq_ref, wo_ref, g1_ref, b1_ref, g_ref, b_ref, o_ref, ob_ref, *, alpha):
    x = _layer_norm_rows(z_ref[...], g1_ref[...], b1_ref[...])
    q = (jnp.dot(x.astype(BF16), wq_ref[...], preferred_element_type=F32) * ATTN_SCALE).astype(BF16)
    width = N_XHEADS * HEAD_DIM
    heads = []
    for h in range(N_XHEADS):
        lo, hi = h * HEAD_DIM, (h + 1) * HEAD_DIM
        s = lax.dot_general(q[:, lo:hi], kv_ref[:, lo:hi], (((1,), (1,)), ((), ())),
                            preferred_element_type=F32)
        p = jnp.exp(s - jnp.max(s, axis=1, keepdims=True))
        o = jnp.dot(p.astype(BF16), kv_ref[:, width + lo:width + hi], preferred_element_type=F32)
        heads.append((o / jnp.sum(p, axis=1, keepdims=True)).astype(BF16))
    f = jnp.dot(jnp.concatenate(heads, axis=1), wo_ref[...], preferred_element_type=F32)
    y = _layer_norm_rows(alpha * x + f, g_ref[...], b_ref[...])
    o_ref[...] = y
    ob_ref[...] = y.astype(ob_ref.dtype)


def _cross_attention(z, kv, wq, wo, layer, g1, b1, g, b, alpha, seq):
    m, d = z.shape
    n_mem = kv.shape[0] // (m // seq)
    width = N_XHEADS * HEAD_DIM
    bm = _tile(seq, 256)
    per_batch = seq // bm
    row_b = pl.BlockSpec((bm, d), lambda i: (i, 0))
    vec = pl.BlockSpec((1, d), lambda i: (0, 0))
    return pl.pallas_call(
        functools.partial(_cross_kernel, alpha=alpha),
        grid=(m // bm,),
        in_specs=[row_b,
                  pl.BlockSpec((n_mem, 2 * width), lambda i: (i // per_batch, 0)),
                  pl.BlockSpec((None, d, width), lambda i: (layer, 0, 0)),
                  pl.BlockSpec((None, width, d), lambda i: (layer, 0, 0)),
                  vec, vec, vec, vec],
        out_specs=[row_b, row_b],
        out_shape=[jax.ShapeDtypeStruct((m, d), F32), jax.ShapeDtypeStruct((m, d), BF16)],
        compiler_params=_params(
            ("parallel",),
            [((bm, d), BF16)] + [((bm, d), F32)] * 2
            + [((n_mem, 2 * width), BF16), ((d, width), BF16), ((width, d), BF16)],
            temps=6 * bm * d * 4),
        name="cross_attention",
    )(z, kv, wq, wo, g1.reshape(1, d), b1.reshape(1, d), g.reshape(1, d), b.reshape(1, d))


def kernel(x, mem, positions, w_in, b_forget, g_fox, g_dil, w_out, ln1_g, ln1_b,
           w_xq, w_xk, w_xv, w_xo, ln2_g, ln2_b, w_up, w_down, ln3_g, ln3_b):
    bsz, seq, d_model = x.shape
    depth = w_in.shape[0]
    nh = d_model // (2 * HEAD_DIM)
    width = nh * HEAD_DIM
    alpha = (2 * depth) ** 0.25
    tokens = bsz * seq
    assert w_in.shape[2] == 6 * width + nh and w_xq.shape[2] == N_XHEADS * HEAD_DIM
    assert seq % MAX_WINDOW == 0

    cos, sa, sb = _rope_tables(positions)
    xf = x.reshape(tokens, d_model)
    xb = xf.astype(BF16)
    memb = mem.reshape(-1, d_model).astype(BF16)

    w_main, w_ft = _prepare_w_in(jnp.swapaxes(w_in, 1, 2), 3 * width, nh)
    w_kv = jnp.concatenate([w_xk, w_xv], axis=2).astype(BF16)
    w_out_b, w_xq_b, w_xo_b = w_out.astype(BF16), w_xq.astype(BF16), w_xo.astype(BF16)
    w_up_l = w_up[0].astype(BF16)

    for l in range(depth):
        qkv = _in_projection(xb, w_main, l, cos, sa, sb, width)
        c = _forget_cumsum(xb, w_ft, l, b_forget[l], bsz, seq)
        y_fox = _fox_attention(qkv, c, g_fox[l], bsz, seq, nh)
        y_dil = _dilated_attention(qkv, g_dil[l], bsz, seq, nh, 3 * nh)
        z = _out_projection(y_fox, y_dil, w_out_b, l, xf, alpha)

        kv = _matmul(memb, w_kv, l, BF16, name="memory_kv")
        xf, xb = _cross_attention(z, kv, w_xq_b, w_xo_b, l, ln1_g[l], ln1_b[l],
                                  ln2_g[l], ln2_b[l], alpha, seq)

        hidden, w_down_l = _matmul(xb, w_up_l, None, BF16, activation="relu2", cast=(w_down, l), name="mlp_up")
        if l + 1 < depth:
            z, w_up_l = _matmul_kacc(hidden, w_down_l, None, xf, alpha, cast=(w_up, l + 1), name="mlp_down")
        else:
            z = _matmul_kacc(hidden, w_down_l, None, xf, alpha, name="mlp_down")
        xf, xb = _layer_norm(z, ln3_g[l], ln3_b[l])

    return xf.reshape(bsz, seq, d_model)
```

```python
import functools
import math

import jax
import jax.numpy as jnp
import numpy as np
from jax import lax
from jax.experimental import pallas as pl
from jax.experimental.pallas import tpu as pltpu

F32 = jnp.float32
BF16 = jnp.bfloat16

HEAD_DIM = 128
ROPE_DIM = HEAD_DIM // 4
ROPE_HALF = ROPE_DIM // 2
ROPE_THETA = 500000.0
DILATION_CONFIGS = ((128, 1), (512, 4), (2048, 16))
MAX_WINDOW = max(w for w, _ in DILATION_CONFIGS)
N_XHEADS = 4
LN_EPS = 1e-5
NEG_INF = -1e30
ATTN_SCALE = HEAD_DIM ** -0.5
LOG2E = math.log2(math.e)
Q_SCALE_LOG2 = ATTN_SCALE * LOG2E

V7X_VMEM_BYTES = 64 * 2**20
LANES = 128


def _nbytes(shape, dtype):
    return math.prod(shape) * jnp.dtype(dtype).itemsize


def _params(semantics, blocks, temps=0):
    need = 2 * sum(_nbytes(s, d) for s, d in blocks) + temps
    limit = min(need + need // 4 + (2 << 20), V7X_VMEM_BYTES - (4 << 20))
    return pltpu.CompilerParams(dimension_semantics=semantics, vmem_limit_bytes=int(limit))


def _tile(dim, pref):
    t = min(dim, pref)
    while dim % t:
        t //= 2
    return t


SEG_Q_FOX, SEG_K_FOX, SEG_V_FOX, SEG_Q_DIL, SEG_K_DIL, SEG_V_DIL = range(6)


def _rope_table_kernel(pos_ref, inv_ref, cos_ref, sa_ref, sb_ref):
    ang = pos_ref[...].astype(F32) * inv_ref[...]
    lane = lax.broadcasted_iota(jnp.int32, ang.shape, 1)
    cos = jnp.cos(ang)
    sin = jnp.sin(ang)
    sa = jnp.where(lane < ROPE_HALF, -sin, 0.0)
    sb = jnp.where((lane >= ROPE_HALF) & (lane < ROPE_DIM), sin, 0.0)
    for ref, t in ((cos_ref, cos), (sa_ref, sa), (sb_ref, sb)):
        ref[0] = t * Q_SCALE_LOG2
        ref[1] = t


def _rope_tables(positions):
    t = positions.size
    rows = _tile(t, 2048)
    inv = ROPE_THETA ** (-jnp.arange(0, ROPE_DIM, 2, dtype=F32) / ROPE_DIM)
    inv_lane = jnp.concatenate([inv, inv, jnp.zeros((LANES - ROPE_DIM,), F32)]).reshape(1, LANES)
    out = jax.ShapeDtypeStruct((2, t, LANES), F32)
    blk = pl.BlockSpec((2, rows, LANES), lambda i: (0, i, 0))
    return pl.pallas_call(
        _rope_table_kernel,
        grid=(t // rows,),
        in_specs=[pl.BlockSpec((rows, 1), lambda i: (i, 0)),
                  pl.BlockSpec((1, LANES), lambda i: (0, 0))],
        out_specs=[blk, blk, blk],
        out_shape=[out, out, out],
        compiler_params=_params(("parallel",), [((rows, LANES), F32)] + [((2, rows, LANES), F32)] * 3,
                                temps=8 * rows * LANES * 4),
        name="rope_tables",
    )(positions.reshape(t, 1), inv_lane)


def _inproj_kernel(x_ref, w_ref, cos_ref, sa_ref, sb_ref, o_ref, *, tiles_per_seg):
    seg = pl.program_id(1) // tiles_per_seg
    rotary = (seg == SEG_Q_DIL) | (seg == SEG_K_DIL)

    def project():
        return lax.dot_general(x_ref[...], w_ref[...], (((1,), (1,)), ((), ())), preferred_element_type=F32)

    @pl.when(rotary)
    def _():
        acc = project()
        cos, sa, sb = cos_ref[...], sa_ref[...], sb_ref[...]
        for c in range(acc.shape[1] // HEAD_DIM):
            cols = slice(c * HEAD_DIM, (c + 1) * HEAD_DIM)
            blk = acc[:, cols]
            out = (blk * cos + pltpu.roll(blk, HEAD_DIM - ROPE_HALF, 1) * sa
                   + pltpu.roll(blk, ROPE_HALF, 1) * sb)
            o_ref[:, cols] = out.astype(o_ref.dtype)

    @pl.when(jnp.logical_not(rotary))
    def _():
        scale = jnp.where(seg == SEG_Q_FOX, Q_SCALE_LOG2, 1.0).astype(F32)
        o_ref[...] = (project() * scale).astype(o_ref.dtype)


def _in_projection(xb, w_t, layer, cos, sa, sb, seg_width):
    m, k = xb.shape
    n = w_t.shape[1]
    bm = _tile(m, 1024)
    bn = _tile(seg_width, 1024)
    tiles_per_seg = seg_width // bn

    def table_set(j):
        return jnp.where(j // tiles_per_seg == SEG_Q_DIL, 0, 1)

    tab = pl.BlockSpec((None, bm, LANES), lambda i, j: (table_set(j), i, 0))
    return pl.pallas_call(
        functools.partial(_inproj_kernel, tiles_per_seg=tiles_per_seg),
        grid=(m // bm, n // bn),
        in_specs=[pl.BlockSpec((bm, k), lambda i, j: (i, 0)),
                  pl.BlockSpec((None, bn, k), lambda i, j: (layer, j, 0)),
                  tab, tab, tab],
        out_specs=pl.BlockSpec((bm, bn), lambda i, j: (i, j)),
        out_shape=jax.ShapeDtypeStruct((m, n), BF16),
        compiler_params=_params(
            ("parallel", "arbitrary"),
            [((bm, k), BF16), ((bn, k), BF16), ((bm, bn), BF16)] + [((bm, LANES), F32)] * 3,
            temps=2 * bm * bn * 4),
        name="in_projection",
    )(xb, w_t, cos, sa, sb)


def _forget_kernel(x_ref, w_ref, b_ref, c_ref, carry_ref):
    @pl.when(pl.program_id(1) == 0)
    def _():
        carry_ref[...] = jnp.zeros_like(carry_ref)

    z = lax.dot_general(w_ref[...], x_ref[...], (((1,), (1,)), ((), ())),
                        preferred_element_type=F32) + b_ref[...]
    log_f = jnp.minimum(z, 0.0) - jnp.log1p(jnp.exp(-jnp.abs(z)))
    ts = z.shape[1]
    row = lax.broadcasted_iota(jnp.int32, (ts, ts), 0)
    col = lax.broadcasted_iota(jnp.int32, (ts, ts), 1)
    upper = (row <= col).astype(BF16)
    hi = log_f.astype(BF16)
    r1 = log_f - hi.astype(F32)
    mid = r1.astype(BF16)
    lo = (r1 - mid.astype(F32)).astype(BF16)
    cum = (jnp.dot(hi, upper, preferred_element_type=F32)
           + jnp.dot(mid, upper, preferred_element_type=F32)
           + jnp.dot(lo, upper, preferred_element_type=F32))
    c = cum + carry_ref[:, 0:1]
    c_ref[...] = c * LOG2E
    carry_ref[...] = jnp.broadcast_to(c[:, ts - 1:ts], carry_ref.shape)


def _forget_cumsum(xb, w_ft, layer, b_f, bsz, seq):
    _, nh, k = w_ft.shape
    ts = _tile(seq, 512)
    ns = seq // ts
    c = pl.pallas_call(
        _forget_kernel,
        grid=(bsz, ns),
        in_specs=[pl.BlockSpec((ts, k), lambda b, s: (b * ns + s, 0)),
                  pl.BlockSpec((None, nh, k), lambda b, s: (layer, 0, 0)),
                  pl.BlockSpec((nh, 1), lambda b, s: (0, 0))],
        out_specs=pl.BlockSpec((None, nh, ts), lambda b, s: (b, 0, s)),
        out_shape=jax.ShapeDtypeStruct((bsz, nh, seq), F32),
        scratch_shapes=[pltpu.VMEM((nh, LANES), F32)],
        compiler_params=_params(
            ("parallel", "arbitrary"),
            [((ts, k), BF16), ((nh, k), BF16), ((nh, ts), F32)],
            temps=4 * ts * ts * 4),
        name="forget_cumsum",
    )(xb, w_ft, b_f.reshape(nh, 1))
    return c.reshape(bsz, nh, 1, seq)


ATTN_CHUNK = 256


ATTN_HEADS = 4
ATTN_TILE = 512


def _head_cols(g):
    return slice(g * HEAD_DIM, (g + 1) * HEAD_DIM)


def _flash_tiles(q_ref, k_ref, v_ref, scratch, n_tiles, key_start, bias, mask_last):
    s_ref, mx_ref, m_ref, accl_ref = scratch
    tq = q_ref.shape[0]
    heads = q_ref.shape[1] // HEAD_DIM
    units = [(g, slice(r, r + ATTN_CHUNK)) for g in range(heads) for r in range(0, tq, ATTN_CHUNK)]
    m_ref[...] = jnp.full_like(m_ref, NEG_INF)
    accl_ref[...] = jnp.zeros_like(accl_ref)

    def keys(t):
        return k_ref[pl.ds(key_start(t), tq), :]

    def values(t):
        v = v_ref[pl.ds(key_start(t), tq), :]
        ones = jnp.ones((tq, HEAD_DIM), v.dtype)
        return [jnp.concatenate([v[:, _head_cols(g)], ones], axis=1) for g in range(heads)]

    def produce(t, k, g, rows, masked):
        s = lax.dot_general(q_ref[rows, _head_cols(g)], k[:, _head_cols(g)],
                            (((1,), (1,)), ((), ())), preferred_element_type=F32) + bias(t, g, rows, masked)
        s_ref[g, rows, :] = s
        mx_ref[g, rows, :] = jnp.broadcast_to(jnp.max(s, axis=1, keepdims=True), (ATTN_CHUNK, LANES))

    def consume(v_aug, g, rows):
        m_prev = m_ref[g, rows, :]
        m_next = jnp.maximum(m_prev, mx_ref[g, rows, :])
        p = jnp.exp2(s_ref[g, rows, :] - jnp.tile(m_next, (1, tq // LANES)))
        alpha = jnp.exp2(m_prev - m_next)
        pv = jnp.dot(p.astype(v_aug[g].dtype), v_aug[g], preferred_element_type=F32)
        accl_ref[g, rows, :] = jnp.tile(alpha, (1, 2)) * accl_ref[g, rows, :] + pv
        m_ref[g, rows, :] = m_next

    def produce_tile(t, masked):
        k = keys(t)
        for g, rows in units:
            produce(t, k, g, rows, masked)

    def advance(t, masked):
        v_aug = values(t)
        k = keys(t + 1)
        for g, rows in units:
            consume(v_aug, g, rows)
            produce(t + 1, k, g, rows, masked)

    def body(t, carry):
        advance(t, False)
        return carry

    if mask_last:
        pl.when(n_tiles > 1)(lambda: produce_tile(0, False))
        pl.when(n_tiles == 1)(lambda: produce_tile(0, True))
        lax.fori_loop(0, n_tiles - 2, body, 0)
        pl.when(n_tiles > 1)(lambda: advance(n_tiles - 2, True))
    else:
        produce_tile(0, False)
        lax.fori_loop(0, n_tiles - 1, body, 0)
    v_aug = values(n_tiles - 1)
    for g, rows in units:
        consume(v_aug, g, rows)


def _finish_heads(accl_ref, g_ref, o_ref):
    for g in range(accl_ref.shape[0]):
        o = accl_ref[g, :, :HEAD_DIM] / accl_ref[g, :, HEAD_DIM:]
        y = o * lax.rsqrt(jnp.mean(o * o, axis=1, keepdims=True) + LN_EPS)
        o_ref[:, _head_cols(g)] = (y * g_ref[:, _head_cols(g)]).astype(o_ref.dtype)


def _attention_call(kernel_fn, qkv, extra, extra_spec, g, bsz, seq, nh, col0, extra_bytes, name):
    tq = _tile(seq, ATTN_TILE)
    nq = seq // tq
    heads = math.gcd(ATTN_HEADS, nh)
    wide = heads * HEAD_DIM
    groups = nh // heads
    first = col0 // heads
    assert col0 % heads == 0 and tq % ATTN_CHUNK == 0
    return pl.pallas_call(
        kernel_fn,
        grid=(bsz, groups, nq),
        in_specs=[pl.BlockSpec((tq, wide), lambda b, h, i: (b * nq + i, first + h)),
                  pl.BlockSpec((seq, wide), lambda b, h, i: (b, first + groups + h)),
                  pl.BlockSpec((seq, wide), lambda b, h, i: (b, first + 2 * groups + h)),
                  extra_spec(heads),
                  pl.BlockSpec((1, wide), lambda b, h, i: (0, h))],
        out_specs=pl.BlockSpec((tq, wide), lambda b, h, i: (b * nq + i, h)),
        out_shape=jax.ShapeDtypeStruct((bsz * seq, nh * HEAD_DIM), BF16),
        scratch_shapes=[pltpu.VMEM((heads, tq, tq), F32),
                        pltpu.VMEM((heads, tq, LANES), F32),
                        pltpu.VMEM((heads, tq, LANES), F32),
                        pltpu.VMEM((heads, tq, 2 * HEAD_DIM), F32)],
        compiler_params=_params(
            ("parallel", "parallel", "arbitrary"),
            [((tq, wide), BF16)] * 2 + [((seq, wide), BF16)] * 2,
            temps=2 * extra_bytes + heads * 10 * tq * tq * 4),
        name=name,
    )(qkv, qkv, qkv, extra, g.reshape(1, nh * HEAD_DIM))


def _fox_kernel(q_ref, k_ref, v_ref, c_ref, g_ref, o_ref, *scratch):
    qi = pl.program_id(2)
    tq = q_ref.shape[0]

    def key_start(t):
        return pl.multiple_of(t * tq, tq)

    def bias(t, g, rows, masked):
        decay = -c_ref[g, :, pl.ds(key_start(t), tq)]
        if not masked:
            return decay
        row = rows.start + lax.broadcasted_iota(jnp.int32, (ATTN_CHUNK, tq), 0)
        col = lax.broadcasted_iota(jnp.int32, (ATTN_CHUNK, tq), 1)
        return jnp.where(col <= row, decay, NEG_INF)

    _flash_tiles(q_ref, k_ref, v_ref, scratch, qi + 1, key_start, bias, mask_last=True)
    _finish_heads(scratch[-1], g_ref, o_ref)


def _fox_attention(qkv, c, g, bsz, seq, nh):
    return _attention_call(
        _fox_kernel, qkv, c,
        lambda heads: pl.BlockSpec((None, heads, 1, seq), lambda b, h, i: (b, h, 0, 0)),
        g, bsz, seq, nh, 0, extra_bytes=ATTN_HEADS * 8 * seq * 4, name="fox_attention")


def _dilated_bias_tables(tq):
    n_off = MAX_WINDOW // tq + 1
    off = np.arange(n_off)[:, None, None] * tq
    dist = off + np.arange(tq)[None, :, None] - np.arange(tq)[None, None, :]
    count = np.zeros(dist.shape, np.float64)
    for window, dilation in DILATION_CONFIGS:
        count += (dist >= 0) & (dist <= window) & (dist % dilation == 0)
    return np.where(count > 0, np.log2(np.maximum(count, 1.0)), NEG_INF).astype(np.float32)


def _dilated_kernel(q_ref, k_ref, v_ref, bias_ref, g_ref, o_ref, *scratch):
    qi = pl.program_id(2)
    tq = q_ref.shape[0]
    n_tiles = jnp.minimum(qi + 1, bias_ref.shape[0])

    def key_start(t):
        return pl.multiple_of((qi - t) * tq, tq)

    def bias(t, g, rows, masked):
        return bias_ref[t, rows, :]

    _flash_tiles(q_ref, k_ref, v_ref, scratch, n_tiles, key_start, bias, mask_last=False)
    _finish_heads(scratch[-1], g_ref, o_ref)


def _dilated_attention(qkv, g, bsz, seq, nh, col0):
    tq = _tile(seq, ATTN_TILE)
    assert MAX_WINDOW % tq == 0
    bias = jnp.asarray(_dilated_bias_tables(tq))
    return _attention_call(
        _dilated_kernel, qkv, bias,
        lambda heads: pl.BlockSpec(bias.shape, lambda b, h, i: (0, 0, 0)),
        g, bsz, seq, nh, col0, extra_bytes=bias.size * 4, name="dilated_attention")


def _outproj_kernel(a_ref, b_ref, wa_ref, wb_ref, x_ref, o_ref, *, alpha):
    o_ref[...] = (alpha * x_ref[...]
                  + jnp.dot(a_ref[...], wa_ref[...], preferred_element_type=F32)
                  + jnp.dot(b_ref[...], wb_ref[...], preferred_element_type=F32))


def _out_projection(ya, yb, w, layer, x, alpha):
    m, ka = ya.shape
    kb = yb.shape[1]
    assert ka == kb and w.shape[1] == ka + kb
    n = w.shape[2]
    bm = _tile(m, 1024)
    bn = _tile(n, 1024)
    return pl.pallas_call(
        functools.partial(_outproj_kernel, alpha=alpha),
        grid=(m // bm, n // bn),
        in_specs=[pl.BlockSpec((bm, ka), lambda i, j: (i, 0)),
                  pl.BlockSpec((bm, kb), lambda i, j: (i, 0)),
                  pl.BlockSpec((None, ka, bn), lambda i, j: (layer, 0, j)),
                  pl.BlockSpec((None, kb, bn), lambda i, j: (layer, 1, j)),
                  pl.BlockSpec((bm, bn), lambda i, j: (i, j))],
        out_specs=pl.BlockSpec((bm, bn), lambda i, j: (i, j)),
        out_shape=jax.ShapeDtypeStruct((m, n), F32),
        compiler_params=_params(
            ("parallel", "arbitrary"),
            [((bm, ka), BF16)] * 2 + [((ka, bn), BF16)] * 2 + [((bm, bn), F32)] * 2,
            temps=bm * bn * 4),
        name="out_projection",
    )(ya, yb, w, w, x)


def _weight_spec(w, layer, block, index):
    if w.ndim == 2:
        return pl.BlockSpec(block, index)
    return pl.BlockSpec((None,) + block, lambda *g: (layer,) + tuple(index(*g)))


def _cast_job(stack, layer, steps, step_index):
    _, rows, cols = stack.shape
    slab = rows // steps
    assert rows % steps == 0 and slab % 16 == 0
    return (pl.BlockSpec((None, slab, cols), lambda *g: (layer, step_index(*g), 0)),
            pl.BlockSpec((slab, cols), lambda *g: (step_index(*g), 0)),
            jax.ShapeDtypeStruct((rows, cols), BF16),
            [((slab, cols), F32), ((slab, cols), BF16)])


def _matmul_kernel(x_ref, w_ref, *rest, activation):
    o_ref = rest[-2] if len(rest) == 3 else rest[0]
    acc = jnp.dot(x_ref[...], w_ref[...], preferred_element_type=F32)
    if activation == "relu2":
        r = jnp.maximum(acc, 0.0)
        acc = r * r
    o_ref[...] = acc.astype(o_ref.dtype)
    if len(rest) == 3:
        rest[2][...] = rest[0][...].astype(BF16)


def _matmul(x, w, layer, out_dtype, activation=None, cast=None, name="matmul"):
    m, k = x.shape
    n = w.shape[-1]
    bm = _tile(m, 1024)
    bn = _tile(n, 1024)
    nj = n // bn
    in_specs = [pl.BlockSpec((bm, k), lambda i, j: (i, 0)),
                _weight_spec(w, layer, (k, bn), lambda i, j: (0, j))]
    out_specs = [pl.BlockSpec((bm, bn), lambda i, j: (i, j))]
    out_shape = [jax.ShapeDtypeStruct((m, n), out_dtype)]
    blocks = [((bm, k), BF16), ((k, bn), BF16), ((bm, bn), out_dtype)]
    args = [x, w]
    if cast is not None:
        c_in, c_out, c_shape, c_blocks = _cast_job(cast[0], cast[1], (m // bm) * nj, lambda i, j: i * nj + j)
        in_specs.append(c_in)
        out_specs.append(c_out)
        out_shape.append(c_shape)
        blocks += c_blocks
        args.append(cast[0])
    out = pl.pallas_call(
        functools.partial(_matmul_kernel, activation=activation),
        grid=(m // bm, nj),
        in_specs=in_specs, out_specs=out_specs, out_shape=out_shape,
        compiler_params=_params(("parallel", "arbitrary"), blocks, temps=2 * bm * bn * 4),
        name=name,
    )(*args)
    return out[0] if cast is None else out


def _matmul_kacc_kernel(h_ref, w_ref, x_ref, *rest, alpha):
    o_ref = rest[-2] if len(rest) == 3 else rest[0]

    def partial_product():
        return jnp.dot(h_ref[...], w_ref[...], preferred_element_type=F32)

    @pl.when(pl.program_id(2) == 0)
    def _():
        o_ref[...] = alpha * x_ref[...] + partial_product()

    @pl.when(pl.program_id(2) != 0)
    def _():
        o_ref[...] += partial_product()

    if len(rest) == 3:
        rest[2][...] = rest[0][...].astype(BF16)


def _matmul_kacc(h, w, layer, x, alpha, cast=None, name="matmul_kacc"):
    m, k = h.shape
    n = w.shape[-1]
    bm = _tile(m, 1024)
    bn = _tile(n, 1024)
    bk = _tile(k, 4096)
    nj, nk = n // bn, k // bk
    in_specs = [pl.BlockSpec((bm, bk), lambda i, j, kk: (i, kk)),
                _weight_spec(w, layer, (bk, bn), lambda i, j, kk: (kk, j)),
                pl.BlockSpec((bm, bn), lambda i, j, kk: (i, j))]
    out_specs = [pl.BlockSpec((bm, bn), lambda i, j, kk: (i, j))]
    out_shape = [jax.ShapeDtypeStruct((m, n), F32)]
    blocks = [((bm, bk), BF16), ((bk, bn), BF16), ((bm, bn), F32), ((bm, bn), F32)]
    args = [h, w, x]
    if cast is not None:
        c_in, c_out, c_shape, c_blocks = _cast_job(cast[0], cast[1], (m // bm) * nj * nk,
                                                   lambda i, j, kk: (i * nj + j) * nk + kk)
        in_specs.append(c_in)
        out_specs.append(c_out)
        out_shape.append(c_shape)
        blocks += c_blocks
        args.append(cast[0])
    out = pl.pallas_call(
        functools.partial(_matmul_kacc_kernel, alpha=alpha),
        grid=(m // bm, nj, nk),
        in_specs=in_specs, out_specs=out_specs, out_shape=out_shape,
        compiler_params=_params(("parallel", "parallel", "arbitrary"), blocks, temps=bm * bn * 4),
        name=name,
    )(*args)
    return out[0] if cast is None else out


def _w_in_kernel(a_ref, b_ref, f_src_ref, o_ref, f_ref, *, first_shifted, skip):
    rb = a_ref.shape[0]

    @pl.when(pl.program_id(1) < first_shifted)
    def _():
        o_ref[...] = a_ref[...].astype(o_ref.dtype)

    @pl.when(pl.program_id(1) >= first_shifted)
    def _():
        o_ref[:rb - skip, :] = a_ref[skip:, :].astype(o_ref.dtype)
        o_ref[rb - skip:, :] = b_ref[...].astype(o_ref.dtype)

    f_ref[...] = f_src_ref[...].astype(f_ref.dtype)


def _prepare_w_in(w_t, split, skip):
    depth, cols, d = w_t.shape
    out_rows = cols - skip
    rb = _tile(split, 512)
    assert split % rb == 0 and out_rows % rb == 0 and rb % skip == 0 and skip % 8 == 0
    per = rb // skip
    return pl.pallas_call(
        functools.partial(_w_in_kernel, first_shifted=split // rb, skip=skip),
        grid=(depth, out_rows // rb),
        in_specs=[pl.BlockSpec((None, rb, d), lambda l, r: (l, r, 0)),
                  pl.BlockSpec((None, skip, d), lambda l, r: (l, (r + 1) * per, 0)),
                  pl.BlockSpec((None, skip, d), lambda l, r: (l, split // skip, 0))],
        out_specs=[pl.BlockSpec((None, rb, d), lambda l, r: (l, r, 0)),
                   pl.BlockSpec((None, skip, d), lambda l, r: (l, 0, 0))],
        out_shape=[jax.ShapeDtypeStruct((depth, out_rows, d), BF16),
                   jax.ShapeDtypeStruct((depth, skip, d), BF16)],
        compiler_params=_params(("parallel", "arbitrary"),
                                [((rb, d), F32), ((rb, d), BF16)] + [((skip, d), F32)] * 3,
                                temps=rb * d * 4),
        name="prepare_w_in",
    )(w_t, w_t, w_t)


def _layer_norm_rows(z, g, b):
    mu = jnp.mean(z, axis=1, keepdims=True)
    d = z - mu
    var = jnp.mean(d * d, axis=1, keepdims=True)
    return d * lax.rsqrt(var + LN_EPS) * g + b


def _ln_kernel(z_ref, g_ref, b_ref, o_ref, ob_ref):
    y = _layer_norm_rows(z_ref[...], g_ref[...], b_ref[...])
    o_ref[...] = y
    ob_ref[...] = y.astype(ob_ref.dtype)


def _layer_norm(z, g, b):
    m, d = z.shape
    bm = _tile(m, 256)
    row = pl.BlockSpec((bm, d), lambda i: (i, 0))
    vec = pl.BlockSpec((1, d), lambda i: (0, 0))
    return pl.pallas_call(
        _ln_kernel,
        grid=(m // bm,),
        in_specs=[row, vec, vec],
        out_specs=[row, row],
        out_shape=[jax.ShapeDtypeStruct((m, d), F32), jax.ShapeDtypeStruct((m, d), BF16)],
        compiler_params=_params(("parallel",), [((bm, d), F32)] * 2 + [((bm, d), BF16)],
                                temps=3 * bm * d * 4),
        name="layer_norm",
    )(z, g.reshape(1, d), b.reshape(1, d))


def _cross_kernel(z_ref, kv_ref, wq_ref, wo_ref, g1_ref, b1_ref, g_ref, b_ref, o_ref, ob_ref, *, alpha):
    x = _layer_norm_rows(z_ref[...], g1_ref[...], b1_ref[...])
    q = (jnp.dot(x.astype(BF16), wq_ref[...], preferred_element_type=F32) * ATTN_SCALE).astype(BF16)
    width = N_XHEADS * HEAD_DIM
    heads = []
    for h in range(N_XHEADS):
        lo, hi = h * HEAD_DIM, (h + 1) * HEAD_DIM
        s = lax.dot_general(q[:, lo:hi], kv_ref[:, lo:hi], (((1,), (1,)), ((), ())),
                            preferred_element_type=F32)
        p = jnp.exp(s - jnp.max(s, axis=1, keepdims=True))
        o = jnp.dot(p.astype(BF16), kv_ref[:, width + lo:width + hi], preferred_element_type=F32)
        heads.append((o / jnp.sum(p, axis=1, keepdims=True)).astype(BF16))
    f = jnp.dot(jnp.concatenate(heads, axis=1), wo_ref[...], preferred_element_type=F32)
    y = _layer_norm_rows(alpha * x + f, g_ref[...], b_ref[...])
    o_ref[...] = y
    ob_ref[...] = y.astype(ob_ref.dtype)


def _cross_attention(z, kv, wq, wo, layer, g1, b1, g, b, alpha, seq):
    m, d = z.shape
    n_mem = kv.shape[0] // (m // seq)
    width = N_XHEADS * HEAD_DIM
    bm = _tile(seq, 256)
    per_batch = seq // bm
    row_b = pl.BlockSpec((bm, d), lambda i: (i, 0))
    vec = pl.BlockSpec((1, d), lambda i: (0, 0))
    return pl.pallas_call(
        functools.partial(_cross_kernel, alpha=alpha),
        grid=(m // bm,),
        in_specs=[row_b,
                  pl.BlockSpec((n_mem, 2 * width), lambda i: (i // per_batch, 0)),
                  pl.BlockSpec((None, d, width), lambda i: (layer, 0, 0)),
                  pl.BlockSpec((None, width, d), lambda i: (layer, 0, 0)),
                  vec, vec, vec, vec],
        out_specs=[row_b, row_b],
        out_shape=[jax.ShapeDtypeStruct((m, d), F32), jax.ShapeDtypeStruct((m, d), BF16)],
        compiler_params=_params(
            ("parallel",),
            [((bm, d), BF16)] + [((bm, d), F32)] * 2
            + [((n_mem, 2 * width), BF16), ((d, width), BF16), ((width, d), BF16)],
            temps=6 * bm * d * 4),
        name="cross_attention",
    )(z, kv, wq, wo, g1.reshape(1, d), b1.reshape(1, d), g.reshape(1, d), b.reshape(1, d))


def kernel(x, mem, positions, w_in, b_forget, g_fox, g_dil, w_out, ln1_g, ln1_b,
           w_xq, w_xk, w_xv, w_xo, ln2_g, ln2_b, w_up, w_down, ln3_g, ln3_b):
    bsz, seq, d_model = x.shape
    depth = w_in.shape[0]
    nh = d_model // (2 * HEAD_DIM)
    width = nh * HEAD_DIM
    alpha = (2 * depth) ** 0.25
    tokens = bsz * seq
    assert w_in.shape[2] == 6 * width + nh and w_xq.shape[2] == N_XHEADS * HEAD_DIM
    assert seq % MAX_WINDOW == 0

    cos, sa, sb = _rope_tables(positions)
    xf = x.reshape(tokens, d_model)
    xb = xf.astype(BF16)
    memb = mem.reshape(-1, d_model).astype(BF16)

    w_main, w_ft = _prepare_w_in(jnp.swapaxes(w_in, 1, 2), 3 * width, nh)
    w_kv = jnp.concatenate([w_xk, w_xv], axis=2).astype(BF16)
    w_out_b, w_xq_b, w_xo_b = w_out.astype(BF16), w_xq.astype(BF16), w_xo.astype(BF16)
    w_up_l = w_up[0].astype(BF16)

    for l in range(depth):
        qkv = _in_projection(xb, w_main, l, cos, sa, sb, width)
        c = _forget_cumsum(xb, w_ft, l, b_forget[l], bsz, seq)
        y_fox = _fox_attention(qkv, c, g_fox[l], bsz, seq, nh)
        y_dil = _dilated_attention(qkv, g_dil[l], bsz, seq, nh, 3 * nh)
        z = _out_projection(y_fox, y_dil, w_out_b, l, xf, alpha)

        kv = _matmul(memb, w_kv, l, BF16, name="memory_kv")
        xf, xb = _cross_attention(z, kv, w_xq_b, w_xo_b, l, ln1_g[l], ln1_b[l],
                                  ln2_g[l], ln2_b[l], alpha, seq)

        hidden, w_down_l = _matmul(xb, w_up_l, None, BF16, activation="relu2", cast=(w_down, l), name="mlp_up")
        if l + 1 < depth:
            z, w_up_l = _matmul_kacc(hidden, w_down_l, None, xf, alpha, cast=(w_up, l + 1), name="mlp_down")
        else:
            z = _matmul_kacc(hidden, w_down_l, None, xf, alpha, name="mlp_down")
        xf, xb = _layer_norm(z, ln3_g[l], ln3_b[l])

    return xf.reshape(bsz, seq, d_model)
```
